```python
import jax, jax.numpy as jnp
from jax import lax
import numpy as np

D_MODEL = 1024
BATCH = 8
SEQ = 4096
DEPTH = 1

MEM_LEN = 256
HG_HEADS = 4
HG_KEY_DIM = 128
HG_VAL_DIM = 128
HG_KW = HG_HEADS * HG_KEY_DIM
HG_VW = HG_HEADS * HG_VAL_DIM
HG_CHUNK = 64
ATT_Q_HEADS = 8
ATT_KV_HEADS = 2
ATT_GROUP = ATT_Q_HEADS // ATT_KV_HEADS
ATT_HEAD_DIM = 64
ATT_QW = ATT_Q_HEADS * ATT_HEAD_DIM
ATT_KVW = ATT_KV_HEADS * ATT_HEAD_DIM
WINDOW = 128
ATT_BLOCK = 128
ROPE_THETA = 500000.0
ROT_DIM = ATT_HEAD_DIM // 4
MEM_HEADS = 4
MEM_HEAD_DIM = D_MODEL // MEM_HEADS
D_FF = 2816
CONV_WIDTH = 3
NORM_EPS = 1e-6
IN_SPLITS = (HG_KW, HG_KW, HG_KW, HG_VW, HG_VW, ATT_QW, ATT_KVW, ATT_KVW, D_MODEL, D_MODEL)
IN_COLS = sum(IN_SPLITS)
SPLIT_IDX = [int(c) for c in np.cumsum(IN_SPLITS)[:-1]]

kernel_name = 'hybrid_hgrn2_swa_memxattn_convffn_encoder'


def rms_norm(x, g):
    xf = x.astype(jnp.float32)
    y = xf * lax.rsqrt(jnp.mean(xf * xf, axis=-1, keepdims=True) + NORM_EPS) * g.astype(jnp.float32)
    return y.astype(x.dtype)


def partial_rotary(t, positions):
    half = ROT_DIM // 2
    inv_freq = 1.0 / (ROPE_THETA ** (jnp.arange(0, ROT_DIM, 2, dtype=jnp.float32) / ROT_DIM))
    ang = positions.astype(jnp.float32)[..., None] * inv_freq
    cos, sin = jnp.cos(ang)[:, :, None, :], jnp.sin(ang)[:, :, None, :]
    tf = t.astype(jnp.float32)
    t1, t2 = tf[..., :half], tf[..., half:ROT_DIM]
    out = jnp.concatenate([t1 * cos - t2 * sin, t2 * cos + t1 * sin, tf[..., ROT_DIM:]], axis=-1)
    return out.astype(t.dtype)


def gla_chunkwise(q, k, log_f, v):
    B, S, H, K = q.shape
    V = v.shape[-1]
    n_chunks = S // HG_CHUNK

    def chunks(t):
        return t.reshape(B, n_chunks, HG_CHUNK, H, t.shape[-1]).transpose(1, 0, 3, 2, 4)

    incl = jnp.tril(jnp.ones((HG_CHUNK, HG_CHUNK), dtype=bool))[:, :, None]

    def step(state, inp):
        qc, kc, gc, vc = inp
        b = jnp.cumsum(gc, axis=2)
        o_inter = jnp.einsum('bhtk,bhkv->bhtv', qc * jnp.exp(b), state)
        decay = jnp.exp(jnp.where(incl, b[:, :, :, None, :] - b[:, :, None, :, :], -jnp.inf))
        scores = jnp.einsum('bhtk,bhtsk->bhts', qc, decay * kc[:, :, None, :, :])
        o_intra = jnp.einsum('bhts,bhsv->bhtv', scores, vc)
        b_end = b[:, :, -1, :]
        new_state = (jnp.exp(b_end)[..., None] * state
                     + jnp.einsum('bhsk,bhsv->bhkv', kc * jnp.exp(b_end[:, :, None, :] - b), vc))
        return new_state, o_inter + o_intra

    state0 = jnp.zeros((B, H, K, V), jnp.float32)
    _, o = lax.scan(step, state0, (chunks(q), chunks(k), chunks(log_f), chunks(v)))
    return o.transpose(1, 0, 3, 2, 4).reshape(B, S, H, V)


def window_gqa_sink(q, k, v, sink):
    B, S = q.shape[0], q.shape[1]
    nb = S // ATT_BLOCK
    qb = q.reshape(B, nb, ATT_BLOCK, ATT_KV_HEADS, ATT_GROUP, ATT_HEAD_DIM)

    def band(t):
        tp = jnp.pad(t.reshape(B, nb, ATT_BLOCK, ATT_KV_HEADS, ATT_HEAD_DIM),
                     ((0, 0), (1, 1), (0, 0), (0, 0), (0, 0)))
        return jnp.concatenate([tp[:, :-2], tp[:, 1:-1], tp[:, 2:]], axis=2)

    kb, vb = band(k), band(v)
    s = jnp.einsum('bnqhgd,bnkhd->bnhgqk', qb, kb).astype(jnp.float32) * (ATT_HEAD_DIM ** -0.5)
    blk = jnp.arange(nb)[:, None, None]
    qi = blk * ATT_BLOCK + jnp.arange(ATT_BLOCK)[None, :, None]
    kj = (blk - 1) * ATT_BLOCK + jnp.arange(3 * ATT_BLOCK)[None, None, :]
    valid = (jnp.abs(qi - kj) <= WINDOW) & (kj >= 0) & (kj < S)
    s = jnp.where(valid[None, :, None, None], s, -jnp.inf)
    sk = sink.astype(jnp.float32).reshape(ATT_KV_HEADS, ATT_GROUP)[None, None, :, :, None, None]
    m = jnp.maximum(jnp.max(s, axis=-1, keepdims=True), sk)
    p = jnp.exp(s - m)
    p = p / (jnp.sum(p, axis=-1, keepdims=True) + jnp.exp(sk - m))
    o = jnp.einsum('bnhgqk,bnkhd->bnqhgd', p.astype(v.dtype), vb)
    return o.reshape(B, S, ATT_QW)


def token_mixers(n, positions, w_in, lb_fwd, lb_bwd, hg_norm, sink, w_br_rec, w_br_att, w_out):
    B, S, _ = n.shape
    f32 = jnp.float32
    proj = n @ w_in
    q_r, fz_f, fz_b, i_r, g_r, q_a, k_a, v_a, gate_r, gate_a = jnp.split(proj, SPLIT_IDX, axis=-1)

    qh = jax.nn.silu(q_r.astype(f32)).reshape(B, S, HG_HEADS, HG_KEY_DIM) * (HG_KEY_DIM ** -0.5)
    vh = i_r.astype(f32).reshape(B, S, HG_HEADS, HG_VAL_DIM)

    def forget(z, lb):
        f = lb + (1.0 - lb) * jax.nn.sigmoid(z.astype(f32))
        return ((1.0 - f).reshape(B, S, HG_HEADS, HG_KEY_DIM),
                jnp.log(f).reshape(B, S, HG_HEADS, HG_KEY_DIM))

    k_f, lf_f = forget(fz_f, lb_fwd)
    k_b, lf_b = forget(fz_b, lb_bwd)
    o_f = gla_chunkwise(qh, k_f, lf_f, vh)
    o_b = gla_chunkwise(qh[:, ::-1], k_b[:, ::-1], lf_b[:, ::-1], vh[:, ::-1])[:, ::-1]
    o = o_f + o_b
    o = (o * lax.rsqrt(jnp.mean(o * o, axis=-1, keepdims=True) + NORM_EPS)
         * hg_norm.astype(f32).reshape(HG_HEADS, HG_VAL_DIM))
    o = o.reshape(B, S, HG_VW) * jax.nn.silu(g_r.astype(f32))
    y_rec = o.astype(n.dtype) @ w_br_rec

    qa = partial_rotary(q_a.reshape(B, S, ATT_Q_HEADS, ATT_HEAD_DIM), positions)
    ka = partial_rotary(k_a.reshape(B, S, ATT_KV_HEADS, ATT_HEAD_DIM), positions)
    va = v_a.reshape(B, S, ATT_KV_HEADS, ATT_HEAD_DIM)
    y_att = window_gqa_sink(qa, ka, va, sink) @ w_br_att

    merged = jax.nn.sigmoid(gate_r) * y_rec + jax.nn.sigmoid(gate_a) * y_att
    return merged @ w_out


def memory_cross_attention(n, mem_n, w_q, w_kv, w_o):
    B, S, _ = n.shape
    M = mem_n.shape[1]
    q = (n @ w_q).reshape(B, S, MEM_HEADS, MEM_HEAD_DIM)
    k, v = jnp.split(mem_n @ w_kv, 2, axis=-1)
    k = k.reshape(B, M, MEM_HEADS, MEM_HEAD_DIM)
    v = v.reshape(B, M, MEM_HEADS, MEM_HEAD_DIM)
    s = jnp.einsum('bshd,bmhd->bhsm', q, k).astype(jnp.float32) * (MEM_HEAD_DIM ** -0.5)
    p = jax.nn.softmax(s, axis=-1).astype(v.dtype)
    o = jnp.einsum('bhsm,bmhd->bshd', p, v).reshape(B, S, D_MODEL)
    return o @ w_o


def conv_ffn(n, w_up, conv_w, conv_b, w_down):
    S = n.shape[1]
    u, g = jnp.split(n @ w_up, 2, axis=-1)
    pad = CONV_WIDTH // 2
    gp = jnp.pad(g, ((0, 0), (pad, pad), (0, 0)))
    gc = sum(gp[:, j:j + S] * conv_w[j] for j in range(CONV_WIDTH)) + conv_b
    return (jax.nn.silu(gc) * u) @ w_down


def setup_inputs(seed: int = 0) -> dict:
    key = jax.random.key(seed)
    ks = jax.random.split(key, 24)

    def nrm(k, shape, scale):
        return jax.random.normal(k, shape, jnp.float32) * scale

    def gain(k, shape):
        return 1.0 + 0.02 * jax.random.normal(k, shape, jnp.float32)

    L, D = DEPTH, D_MODEL
    return {
        'x': nrm(ks[0], (BATCH, SEQ, D), 1.0),
        'mem': nrm(ks[1], (BATCH, MEM_LEN, D), 1.0),
        'positions': jnp.broadcast_to(jnp.arange(SEQ, dtype=jnp.int32), (BATCH, SEQ)),
        'norm_mix': gain(ks[2], (L, D)),
        'w_in': nrm(ks[3], (L, D, IN_COLS), D ** -0.5),
        'lower_bounds': nrm(ks[4], (2, L + 1, HG_KW), 0.5),
        'hg_norm': gain(ks[5], (L, HG_VW)),
        'attn_sink': nrm(ks[6], (L, ATT_Q_HEADS), 0.5),
        'w_br_rec': nrm(ks[7], (L, HG_VW, D), HG_VW ** -0.5),
        'w_br_att': nrm(ks[8], (L, ATT_QW, D), ATT_QW ** -0.5),
        'w_mix_out': nrm(ks[9], (L, D, D), D ** -0.5),
        'norm_mem': gain(ks[10], (L, D)),
        'norm_mem_kv': gain(ks[11], (L, D)),
        'w_mem_q': nrm(ks[12], (L, D, D), D ** -0.5),
        'w_mem_kv': nrm(ks[13], (L, D, 2 * D), D ** -0.5),
        'w_mem_o': nrm(ks[14], (L, D, D), D ** -0.5),
        'norm_ffn': gain(ks[15], (L, D)),
        'w_up': nrm(ks[16], (L, D, 2 * D_FF), D ** -0.5),
        'conv_w': nrm(ks[17], (L, CONV_WIDTH, D_FF), CONV_WIDTH ** -0.5),
        'conv_b': nrm(ks[18], (L, D_FF), 0.01),
        'w_down': nrm(ks[19], (L, D_FF, D), D_FF ** -0.5),
        'final_norm': gain(ks[20], (D,)),
    }


def reference(x, mem, positions, norm_mix, w_in, lower_bounds, hg_norm, attn_sink, w_br_rec,
              w_br_att, w_mix_out, norm_mem, norm_mem_kv, w_mem_q, w_mem_kv, w_mem_o, norm_ffn,
              w_up, conv_w, conv_b, w_down, final_norm):
    lb_table = jnp.cumsum(jax.nn.softmax(lower_bounds.astype(jnp.float32), axis=1), axis=1)
    h = x
    for l in range(DEPTH):
        n = rms_norm(h, norm_mix[l])
        h = h + token_mixers(n, positions, w_in[l], lb_table[0, l], lb_table[1, l], hg_norm[l],
                             attn_sink[l], w_br_rec[l], w_br_att[l], w_mix_out[l])
        n = rms_norm(h, norm_mem[l])
        mem_n = rms_norm(mem, norm_mem_kv[l])
        h = h + memory_cross_attention(n, mem_n, w_mem_q[l], w_mem_kv[l], w_mem_o[l])
        n = rms_norm(h, norm_ffn[l])
        h = h + conv_ffn(n, w_up[l], conv_w[l], conv_b[l], w_down[l])
    return rms_norm(h, final_norm)
```

```python
import functools

import jax
import jax.numpy as jnp
from jax import lax
from jax.experimental import pallas as pl
from jax.experimental.pallas import tpu as pltpu

F32 = jnp.float32
BF16 = jnp.bfloat16

D_MODEL = 1024
MEM_LEN = 256
HG_HEADS = 4
HG_DIM = 128
HG_W = HG_HEADS * HG_DIM
HG_CHUNK = 64
HG_MID = HG_CHUNK // 2
ATT_Q_HEADS = 8
ATT_KV_HEADS = 2
ATT_HEAD_DIM = 64
ATT_QW = ATT_Q_HEADS * ATT_HEAD_DIM
ATT_KVW = ATT_KV_HEADS * ATT_HEAD_DIM
WINDOW = 128
ATT_BLOCK = 128
ROPE_THETA = 500000.0
ROT_DIM = ATT_HEAD_DIM // 4
ROT_HALF = ROT_DIM // 2
MEM_HEADS = 4
MEM_HEAD_DIM = D_MODEL // MEM_HEADS
D_FF = 2816
NORM_EPS = 1e-6

LANES = 128
SUBLANES_F32 = 8
VMEM_LIMIT_BYTES = 56 * 1024 * 1024

SEQ_TILE = 512
FF_CHUNKS = (768, 768, 768, 512)


def _const_spec(shape):
    nd = len(shape)
    return pl.BlockSpec(shape, lambda *_: (0,) * nd, pipeline_mode=pl.Buffered(1))


def _rms(x, g):
    ms = jnp.mean(x * x, axis=-1, keepdims=True)
    return x * lax.rsqrt(ms + NORM_EPS) * g


def _sigmoid(x):
    return 1.0 / (1.0 + jnp.exp(-x))


def _dot(a, b):
    return jnp.dot(a, b, preferred_element_type=F32)


def _dot_nt(a, b):
    return lax.dot_general(a, b, (((1,), (1,)), ((), ())), preferred_element_type=F32)


def _dot_tn(a, b):
    return lax.dot_general(a, b, (((0,), (0,)), ((), ())), preferred_element_type=F32)


def _rope_tables(pos_col):
    lane = lax.broadcasted_iota(jnp.int32, (1, LANES), 1)
    d = lane % ATT_HEAD_DIM
    half = ROT_HALF
    fidx = d % half
    inv_freq = jnp.zeros((1, LANES), F32)
    for i in range(half):
        inv_freq = jnp.where(fidx == i, 1.0 / (ROPE_THETA ** (2.0 * i / ROT_DIM)), inv_freq)
    ang = pos_col.astype(F32) * inv_freq
    cos = jnp.cos(ang)
    sin = jnp.sin(ang)
    lo = d < half
    hi = (d >= half) & (d < ROT_DIM)
    c = jnp.where(d < ROT_DIM, cos, 1.0)
    s_lo = jnp.where(lo, -sin, 0.0)
    s_hi = jnp.where(hi, sin, 0.0)
    return c, s_lo, s_hi


def _rope_apply(t, c, s_lo, s_hi):
    up = pltpu.roll(t, LANES - ROT_HALF, axis=1)
    dn = pltpu.roll(t, ROT_HALF, axis=1)
    return t * c + up * s_lo + dn * s_hi


def _lower_bound(lbp_ref):
    p = lbp_ref[...]
    m = jnp.max(p, axis=0, keepdims=True)
    e = jnp.exp(p - m)
    return e[0:1, :] / jnp.sum(e, axis=0, keepdims=True)


def _hgrn_prepare(q_r, fz, i_r, lb, q_s, lf_s, k_s, v_s):
    q_s[...] = (q_r * _sigmoid(q_r)) * (HG_DIM ** -0.5)
    f = lb + (1.0 - lb) * _sigmoid(fz)
    k_s[...] = 1.0 - f
    lf_s[...] = jnp.log(f)
    v_s[...] = i_r.astype(BF16)


def _hgrn_scan(q_s, lf_s, k_s, v_s, o_ref, st_s, n_chunks, reverse):
    row = lax.broadcasted_iota(jnp.int32, (HG_CHUNK, HG_CHUNK), 0)
    col = lax.broadcasted_iota(jnp.int32, (HG_CHUNK, HG_CHUNK), 1)
    keep = (col >= row) if reverse else (col <= row)
    tri = jnp.where(keep, 1.0, 0.0).astype(BF16)
    end_row = 0 if reverse else HG_CHUNK - 1

    def body(step, carry):
        c = (n_chunks - 1 - step) if reverse else step
        r0 = pl.multiple_of(c * HG_CHUNK, HG_CHUNK)
        rows = pl.ds(r0, HG_CHUNK)
        lf = lf_s[rows, :]
        hi = lf.astype(BF16)
        r1 = lf - hi.astype(F32)
        mid = r1.astype(BF16)
        lo = (r1 - mid.astype(F32)).astype(BF16)
        cs = _dot(tri, hi) + _dot(tri, mid) + _dot(tri, lo)
        c_end = cs[end_row:end_row + 1, :]
        c_mid = cs[HG_MID:HG_MID + 1, :]
        q = q_s[rows, :]
        k = k_s[rows, :]
        v = v_s[rows, :]
        q_in = (q * jnp.exp(cs - c_mid)).astype(BF16)
        k_in = (k * jnp.exp(c_mid - cs)).astype(BF16)
        q_st = (q * jnp.exp(cs)).astype(BF16)
        k_st = (k * jnp.exp(c_end - cs)).astype(BF16)
        dec = jnp.exp(c_end)
        for h in range(HG_HEADS):
            sl = slice(h * HG_DIM, (h + 1) * HG_DIM)
            a = _dot_nt(q_in[:, sl], k_in[:, sl])
            a = jnp.where(keep, a, 0.0).astype(BF16)
            st = st_s[h]
            o = _dot(a, v[:, sl]) + _dot_nt(q_st[:, sl], st.astype(BF16))
            o_ref[rows, sl] = o
            st_s[h] = st * dec[:, sl] + _dot_tn(v[:, sl], k_st[:, sl])
        return carry

    lax.fori_loop(0, n_chunks, body, 0)


def _mem_kv_kernel(mem_ref, g_ref, w_ref, k_ref, v_ref):
    n = _rms(mem_ref[0], g_ref[...]).astype(BF16)
    kv = _dot(n, w_ref[...])
    k_ref[0] = kv[:, :D_MODEL].astype(BF16)
    v_ref[0] = kv[:, D_MODEL:].astype(BF16)


def _mem_kv(mem, g, w_kv):
    b = mem.shape[0]
    blk = pl.BlockSpec((1, MEM_LEN, D_MODEL), lambda i: (i, 0, 0))
    return pl.pallas_call(
        _mem_kv_kernel,
        grid=(b,),
        in_specs=[blk, _const_spec((1, D_MODEL)), _const_spec((D_MODEL, 2 * D_MODEL))],
        out_specs=[blk, blk],
        out_shape=[jax.ShapeDtypeStruct((b, MEM_LEN, D_MODEL), BF16)] * 2,
        compiler_params=pltpu.CompilerParams(
            dimension_semantics=("arbitrary",), vmem_limit_bytes=VMEM_LIMIT_BYTES),
        name="mem_kv",
    )(mem, g, w_kv)


BWD_COLS = 3 * HG_W + 2 * ATT_KVW


def _mix_bwd_kernel(x_ref, pos_ref, g_ref, w_ref, lbp_ref,
                    ob_ref, kk_ref, vv_ref,
                    q_s, lf_s, k_s, v_s, st_s):
    @pl.when(pl.program_id(1) == 0)
    def _():
        st_s[...] = jnp.zeros_like(st_s)

    n = _rms(x_ref[0], g_ref[...]).astype(BF16)
    proj = _dot(n, w_ref[...])
    lb = _lower_bound(lbp_ref)
    _hgrn_prepare(proj[:, 0:HG_W], proj[:, HG_W:2 * HG_W], proj[:, 2 * HG_W:3 * HG_W], lb,
                  q_s, lf_s, k_s, v_s)

    c, s_lo, s_hi = _rope_tables(pos_ref[0])
    k_a = _rope_apply(proj[:, 3 * HG_W:3 * HG_W + ATT_KVW], c, s_lo, s_hi)
    v_a = proj[:, 3 * HG_W + ATT_KVW:3 * HG_W + 2 * ATT_KVW]
    lane = lax.broadcasted_iota(jnp.int32, (1, LANES), 1)
    first = lane < ATT_HEAD_DIM
    for src, dst in ((k_a, kk_ref), (v_a, vv_ref)):
        sw = pltpu.roll(src, ATT_HEAD_DIM, axis=1)
        dst[0, :, 0 * LANES:1 * LANES] = jnp.where(first, src, 0.0).astype(BF16)
        dst[0, :, 1 * LANES:2 * LANES] = jnp.where(first, 0.0, sw).astype(BF16)
        dst[0, :, 2 * LANES:3 * LANES] = jnp.where(first, sw, 0.0).astype(BF16)
        dst[0, :, 3 * LANES:4 * LANES] = jnp.where(first, 0.0, src).astype(BF16)

    _hgrn_scan(q_s, lf_s, k_s, v_s, ob_ref.at[0], st_s, SEQ_TILE // HG_CHUNK, reverse=True)


def _mix_bwd(x, pos3, g, w, lbp):
    b, s, _ = x.shape
    nt = s // SEQ_TILE
    rev = lambda bi, i: (bi, nt - 1 - i, 0)
    return pl.pallas_call(
        _mix_bwd_kernel,
        grid=(b, nt),
        in_specs=[
            pl.BlockSpec((1, SEQ_TILE, D_MODEL), rev),
            pl.BlockSpec((1, SEQ_TILE, 1), rev),
            _const_spec((1, D_MODEL)),
            _const_spec((D_MODEL, BWD_COLS)),
            _const_spec(lbp.shape),
        ],
        out_specs=[
            pl.BlockSpec((1, SEQ_TILE, HG_W), rev),
            pl.BlockSpec((1, SEQ_TILE, 4 * LANES), rev),
            pl.BlockSpec((1, SEQ_TILE, 4 * LANES), rev),
        ],
        out_shape=[
            jax.ShapeDtypeStruct((b, s, HG_W), F32),
            jax.ShapeDtypeStruct((b, s, 4 * LANES), BF16),
            jax.ShapeDtypeStruct((b, s, 4 * LANES), BF16),
        ],
        scratch_shapes=[
            pltpu.VMEM((SEQ_TILE, HG_W), F32),
            pltpu.VMEM((SEQ_TILE, HG_W), F32),
            pltpu.VMEM((SEQ_TILE, HG_W), F32),
            pltpu.VMEM((SEQ_TILE, HG_W), BF16),
            pltpu.VMEM((HG_HEADS, HG_DIM, HG_DIM), F32),
        ],
        compiler_params=pltpu.CompilerParams(
            dimension_semantics=("arbitrary", "arbitrary"), vmem_limit_bytes=VMEM_LIMIT_BYTES),
        name="mix_bwd",
    )(x, pos3, g, w, lbp)


FWD_COLS = 4 * HG_W + ATT_QW + 2 * D_MODEL
Q_BLOCKS = SEQ_TILE // ATT_BLOCK
KEY_SPAN = 3 * ATT_BLOCK


def _mix_fwd_kernel(sink_ref, x_ref, pos_ref, ob_ref,
                    kk_prev, kk_main, kk_next, vv_prev, vv_main, vv_next,
                    g_ref, w_ref, lbp_ref, hgn_ref, wrec_ref, watt_ref, wout_ref,
                    h_ref,
                    proj_s, q_s, lf_s, k_s, v_s, of_s, qrot_s, att_s, st_s):
    ti = pl.program_id(1)

    @pl.when(ti == 0)
    def _():
        st_s[...] = jnp.zeros_like(st_s)

    x = x_ref[0]
    n = _rms(x, g_ref[...]).astype(BF16)
    proj_s[...] = _dot(n, w_ref[...])

    lb = _lower_bound(lbp_ref)
    _hgrn_prepare(proj_s[:, 0:HG_W], proj_s[:, HG_W:2 * HG_W], proj_s[:, 2 * HG_W:3 * HG_W], lb,
                  q_s, lf_s, k_s, v_s)
    _hgrn_scan(q_s, lf_s, k_s, v_s, of_s, st_s, SEQ_TILE // HG_CHUNK, reverse=False)
    o = of_s[...] + ob_ref[0]
    g_r = proj_s[:, 3 * HG_W:4 * HG_W]
    gate = g_r * _sigmoid(g_r)
    hgn = hgn_ref[...]
    parts = []
    for h in range(HG_HEADS):
        sl = slice(h * HG_DIM, (h + 1) * HG_DIM)
        oh = o[:, sl]
        ms = jnp.mean(oh * oh, axis=-1, keepdims=True)
        parts.append((oh * lax.rsqrt(ms + NORM_EPS) * hgn[:, sl] * gate[:, sl]).astype(BF16))
    y_rec = _dot(jnp.concatenate(parts, axis=1), wrec_ref[...])

    c, s_lo, s_hi = _rope_tables(pos_ref[0])
    q0 = 4 * HG_W
    for j in range(ATT_QW // LANES):
        t = proj_s[:, q0 + j * LANES:q0 + (j + 1) * LANES]
        qrot_s[:, j * LANES:(j + 1) * LANES] = (
            _rope_apply(t, c, s_lo, s_hi) * (ATT_HEAD_DIM ** -0.5)).astype(BF16)

    kk_all = jnp.concatenate([kk_prev[0], kk_main[0], kk_next[0]], axis=0)
    vv_all = jnp.concatenate([vv_prev[0], vv_main[0], vv_next[0]], axis=0)
    r_i = lax.broadcasted_iota(jnp.int32, (ATT_BLOCK, KEY_SPAN), 0)
    c_i = lax.broadcasted_iota(jnp.int32, (ATT_BLOCK, KEY_SPAN), 1)
    off = c_i - r_i
    lane = lax.broadcasted_iota(jnp.int32, (1, LANES), 1)
    first = lane < ATT_HEAD_DIM
    seq_len = pl.num_programs(1) * SEQ_TILE
    for j in range(Q_BLOCKS):
        kstart = ti * SEQ_TILE + (j - 1) * ATT_BLOCK
        kpos = c_i + kstart
        valid = (off >= 0) & (off <= 2 * WINDOW) & (kpos >= 0) & (kpos < seq_len)
        bias = jnp.where(valid, 0.0, -jnp.inf)
        rows = slice(j * ATT_BLOCK, (j + 1) * ATT_BLOCK)
        keys = slice(j * ATT_BLOCK, j * ATT_BLOCK + KEY_SPAN)
        for g in range(ATT_KV_HEADS):
            ca = slice(2 * g * LANES, (2 * g + 1) * LANES)
            cb = slice((2 * g + 1) * LANES, (2 * g + 2) * LANES)
            lhs = jnp.concatenate([qrot_s[rows, ca], qrot_s[rows, cb]], axis=0)
            k_bd = jnp.concatenate([kk_all[keys, ca], kk_all[keys, cb]], axis=0)
            v_bd = jnp.concatenate([vv_all[keys, ca], vv_all[keys, cb]], axis=0)
            s = _dot_nt(lhs, k_bd)
            p_rows, inv_rows = [], []
            for half in range(2):
                p_seg, inv_seg = [], []
                for seg in range(2):
                    head = 4 * g + 2 * half + seg
                    ss = s[half * ATT_BLOCK:(half + 1) * ATT_BLOCK,
                           seg * KEY_SPAN:(seg + 1) * KEY_SPAN] + bias
                    sk = sink_ref[head]
                    m = jnp.maximum(jnp.max(ss, axis=-1, keepdims=True), sk)
                    p = jnp.exp(ss - m)
                    den = jnp.sum(p, axis=-1, keepdims=True) + jnp.exp(sk - m)
                    p_seg.append(p.astype(BF16))
                    inv_seg.append(1.0 / den)
                p_rows.append(jnp.concatenate(p_seg, axis=1))
                inv_rows.append(jnp.where(first, inv_seg[0], inv_seg[1]))
            pv = _dot(jnp.concatenate(p_rows, axis=0), v_bd)
            att_s[rows, ca] = (pv[:ATT_BLOCK] * inv_rows[0]).astype(BF16)
            att_s[rows, cb] = (pv[ATT_BLOCK:] * inv_rows[1]).astype(BF16)
    y_att = _dot(att_s[...], watt_ref[...])

    g0 = 4 * HG_W + ATT_QW
    merged = (_sigmoid(proj_s[:, g0:g0 + D_MODEL]) * y_rec
              + _sigmoid(proj_s[:, g0 + D_MODEL:g0 + 2 * D_MODEL]) * y_att)
    h_ref[0] = x + _dot(merged.astype(BF16), wout_ref[...])


def _mix_fwd(sink, x, pos3, ob, kk, vv, g, w, lbp, hgn, wrec, watt, wout):
    b, s, _ = x.shape
    nt = s // SEQ_TILE
    r = SEQ_TILE // ATT_BLOCK
    nblk = s // ATT_BLOCK
    main = lambda bi, i: (bi, i, 0)
    prev = lambda bi, i: (bi, jnp.maximum(i * r - 1, 0), 0)
    nxt = lambda bi, i: (bi, jnp.minimum((i + 1) * r, nblk - 1), 0)
    halo = lambda imap: pl.BlockSpec((1, ATT_BLOCK, 4 * LANES), imap)
    tile = lambda w_: pl.BlockSpec((1, SEQ_TILE, w_), main)
    return pl.pallas_call(
        _mix_fwd_kernel,
        grid=(b, nt),
        in_specs=[
            pl.BlockSpec(memory_space=pltpu.SMEM),
            tile(D_MODEL), tile(1), tile(HG_W),
            halo(prev), tile(4 * LANES), halo(nxt),
            halo(prev), tile(4 * LANES), halo(nxt),
            _const_spec((1, D_MODEL)),
            _const_spec((D_MODEL, FWD_COLS)),
            _const_spec(lbp.shape),
            _const_spec((1, HG_W)),
            _const_spec((HG_W, D_MODEL)),
            _const_spec((ATT_QW, D_MODEL)),
            _const_spec((D_MODEL, D_MODEL)),
        ],
        out_specs=tile(D_MODEL),
        out_shape=jax.ShapeDtypeStruct((b, s, D_MODEL), F32),
        scratch_shapes=[
            pltpu.VMEM((SEQ_TILE, FWD_COLS), F32),
            pltpu.VMEM((SEQ_TILE, HG_W), F32),
            pltpu.VMEM((SEQ_TILE, HG_W), F32),
            pltpu.VMEM((SEQ_TILE, HG_W), F32),
            pltpu.VMEM((SEQ_TILE, HG_W), BF16),
            pltpu.VMEM((SEQ_TILE, HG_W), F32),
            pltpu.VMEM((SEQ_TILE, ATT_QW), BF16),
            pltpu.VMEM((SEQ_TILE, ATT_QW), BF16),
            pltpu.VMEM((HG_HEADS, HG_DIM, HG_DIM), F32),
        ],
        compiler_params=pltpu.CompilerParams(
            dimension_semantics=("arbitrary", "arbitrary"), vmem_limit_bytes=VMEM_LIMIT_BYTES),
        name="mix_fwd",
    )(sink, x, pos3, ob, kk, kk, kk, vv, vv, vv, g, w, lbp, hgn, wrec, watt, wout)


def _mem_xattn_kernel(h_ref, k_ref, v_ref, g_ref, wq_ref, wo_ref, o_ref, att_s):
    h = h_ref[0]
    n = _rms(h, g_ref[...]).astype(BF16)
    q = (_dot(n, wq_ref[...]) * (MEM_HEAD_DIM ** -0.5)).astype(BF16)
    for hd in range(MEM_HEADS):
        sl = slice(hd * MEM_HEAD_DIM, (hd + 1) * MEM_HEAD_DIM)
        s = _dot_nt(q[:, sl], k_ref[0, :, sl])
        m = jnp.max(s, axis=-1, keepdims=True)
        p = jnp.exp(s - m)
        inv = 1.0 / jnp.sum(p, axis=-1, keepdims=True)
        att_s[:, sl] = (_dot(p.astype(BF16), v_ref[0, :, sl]) * inv).astype(BF16)
    o_ref[0] = h + _dot(att_s[...], wo_ref[...])


def _mem_xattn(h, k_mem, v_mem, g, wq, wo):
    b, s, _ = h.shape
    tile = pl.BlockSpec((1, SEQ_TILE, D_MODEL), lambda bi, i: (bi, i, 0))
    kv = pl.BlockSpec((1, MEM_LEN, D_MODEL), lambda bi, i: (bi, 0, 0))
    return pl.pallas_call(
        _mem_xattn_kernel,
        grid=(b, s // SEQ_TILE),
        in_specs=[tile, kv, kv, _const_spec((1, D_MODEL)),
                  _const_spec((D_MODEL, D_MODEL)), _const_spec((D_MODEL, D_MODEL))],
        out_specs=tile,
        out_shape=jax.ShapeDtypeStruct((b, s, D_MODEL), F32),
        scratch_shapes=[pltpu.VMEM((SEQ_TILE, D_MODEL), BF16)],
        compiler_params=pltpu.CompilerParams(
            dimension_semantics=("arbitrary", "arbitrary"), vmem_limit_bytes=VMEM_LIMIT_BYTES),
        name="mem_xattn",
    )(h, k_mem, v_mem, g, wq, wo)


FFN_ROWS = SEQ_TILE + 2 * SUBLANES_F32


def _conv_ffn_kernel(hp_ref, h_ref, hn_ref, g_ref, wu_ref, wg_ref, cw_ref, cb_ref, wd_ref, fn_ref,
                     o_ref):
    ti = pl.program_id(1)
    h = h_ref[0]
    hcat = jnp.concatenate([hp_ref[0], h, hn_ref[0]], axis=0)
    nf = _rms(hcat, g_ref[...])
    n_all = nf.astype(BF16)
    n_main = nf[SUBLANES_F32:SUBLANES_F32 + SEQ_TILE].astype(BF16)
    ridx = lax.broadcasted_iota(jnp.int32, (FFN_ROWS, 1), 0)
    pad_lo = (ridx < SUBLANES_F32) & (ti == 0)
    pad_hi = (ridx >= SUBLANES_F32 + SEQ_TILE) & (ti == pl.num_programs(1) - 1)
    rowmask = jnp.where(pad_lo | pad_hi, 0.0, 1.0)
    main = slice(SUBLANES_F32, SUBLANES_F32 + SEQ_TILE)
    acc = jnp.zeros((SEQ_TILE, D_MODEL), F32)
    c0 = 0
    for width in FF_CHUNKS:
        cols = slice(c0, c0 + width)
        u = _dot(n_main, wu_ref[:, cols])
        gx = _dot(n_all, wg_ref[:, cols]) * rowmask
        g_prev = pltpu.roll(gx, 1, axis=0)[main]
        g_next = pltpu.roll(gx, FFN_ROWS - 1, axis=0)[main]
        gc = (g_prev * cw_ref[0:1, cols] + gx[main] * cw_ref[1:2, cols]
              + g_next * cw_ref[2:3, cols] + cb_ref[:, cols])
        a = (gc * _sigmoid(gc) * u).astype(BF16)
        acc = acc + _dot(a, wd_ref[cols, :])
        c0 += width
    o_ref[0] = _rms(h + acc, fn_ref[...])


def _conv_ffn(h, g, wu, wg, cw, cb, wd, fn):
    b, s, _ = h.shape
    r = SEQ_TILE // SUBLANES_F32
    nblk = s // SUBLANES_F32
    tile = pl.BlockSpec((1, SEQ_TILE, D_MODEL), lambda bi, i: (bi, i, 0))
    prev = pl.BlockSpec((1, SUBLANES_F32, D_MODEL), lambda bi, i: (bi, jnp.maximum(i * r - 1, 0), 0))
    nxt = pl.BlockSpec((1, SUBLANES_F32, D_MODEL),
                       lambda bi, i: (bi, jnp.minimum((i + 1) * r, nblk - 1), 0))
    return pl.pallas_call(
        _conv_ffn_kernel,
        grid=(b, s // SEQ_TILE),
        in_specs=[prev, tile, nxt, _const_spec((1, D_MODEL)),
                  _const_spec((D_MODEL, D_FF)), _const_spec((D_MODEL, D_FF)),
                  _const_spec((3, D_FF)), _const_spec((1, D_FF)),
                  _const_spec((D_FF, D_MODEL)), _const_spec((1, D_MODEL))],
        out_specs=tile,
        out_shape=jax.ShapeDtypeStruct((b, s, D_MODEL), F32),
        compiler_params=pltpu.CompilerParams(
            dimension_semantics=("arbitrary", "arbitrary"), vmem_limit_bytes=VMEM_LIMIT_BYTES),
        name="conv_ffn",
    )(h, h, h, g, wu, wg, cw, cb, wd, fn)


def kernel(x, mem, positions, norm_mix, w_in, lower_bounds, hg_norm, attn_sink, w_br_rec, w_br_att,
           w_mix_out, norm_mem, norm_mem_kv, w_mem_q, w_mem_kv, w_mem_o, norm_ffn, w_up, conv_w,
           conv_b, w_down, final_norm):
    assert x.shape[1] % SEQ_TILE == 0 and x.shape[2] == D_MODEL and w_in.shape[0] == 1
    bf = lambda a: a.astype(BF16)
    row = lambda a: a.reshape(1, -1).astype(F32)

    o = 0
    cols = {}
    for name, width in (("q_r", HG_W), ("fz_f", HG_W), ("fz_b", HG_W), ("i_r", HG_W), ("g_r", HG_W),
                        ("q_a", ATT_QW), ("k_a", ATT_KVW), ("v_a", ATT_KVW),
                        ("gate_r", D_MODEL), ("gate_a", D_MODEL)):
        cols[name] = w_in[0][:, o:o + width]
        o += width
    w_bwd = bf(jnp.concatenate([cols[k] for k in ("q_r", "fz_b", "i_r", "k_a", "v_a")], axis=1))
    w_fwd = bf(jnp.concatenate(
        [cols[k] for k in ("q_r", "fz_f", "i_r", "g_r", "q_a", "gate_r", "gate_a")], axis=1))

    pos3 = positions.reshape(positions.shape[0], positions.shape[1], 1)
    g_mix = row(norm_mix[0])

    k_mem, v_mem = _mem_kv(mem, row(norm_mem_kv[0]), bf(w_mem_kv[0]))
    o_bwd, kk, vv = _mix_bwd(x, pos3, g_mix, w_bwd, lower_bounds[1].astype(F32))
    h1 = _mix_fwd(attn_sink[0].astype(F32), x, pos3, o_bwd, kk, vv, g_mix, w_fwd,
                  lower_bounds[0].astype(F32), row(hg_norm[0]), bf(w_br_rec[0]), bf(w_br_att[0]),
                  bf(w_mix_out[0]))
    h2 = _mem_xattn(h1, k_mem, v_mem, row(norm_mem[0]), bf(w_mem_q[0]), bf(w_mem_o[0]))
    return _conv_ffn(h2, row(norm_ffn[0]), bf(w_up[0][:, :D_FF]), bf(w_up[0][:, D_FF:]),
                     conv_w[0].astype(F32), row(conv_b[0]), bf(w_down[0]), row(final_norm))
```

```python
import functools

import jax
import jax.numpy as jnp
from jax import lax
from jax.experimental import pallas as pl
from jax.experimental.pallas import tpu as pltpu

F32 = jnp.float32
BF16 = jnp.bfloat16

D_MODEL = 1024
MEM_LEN = 256
HG_HEADS = 4
HG_DIM = 128
HG_W = HG_HEADS * HG_DIM
HG_CHUNK = 64
HG_MID = HG_CHUNK // 2
ATT_Q_HEADS = 8
ATT_KV_HEADS = 2
ATT_HEAD_DIM = 64
ATT_QW = ATT_Q_HEADS * ATT_HEAD_DIM
ATT_KVW = ATT_KV_HEADS * ATT_HEAD_DIM
WINDOW = 128
ATT_BLOCK = 128
ROPE_THETA = 500000.0
ROT_DIM = ATT_HEAD_DIM // 4
ROT_HALF = ROT_DIM // 2
MEM_HEADS = 4
MEM_HEAD_DIM = D_MODEL // MEM_HEADS
D_FF = 2816
NORM_EPS = 1e-6

LANES = 128
SUBLANES_F32 = 8
VMEM_LIMIT_BYTES = 56 * 1024 * 1024

SEQ_TILE = 512
FF_CHUNKS = (768, 768, 768, 512)


def _const_spec(shape):
    nd = len(shape)
    return pl.BlockSpec(shape, lambda *_: (0,) * nd, pipeline_mode=pl.Buffered(1))


def _rms(x, g):
    ms = jnp.mean(x * x, axis=-1, keepdims=True)
    return x * lax.rsqrt(ms + NORM_EPS) * g


def _sigmoid(x):
    return 1.0 / (1.0 + jnp.exp(-x))


def _dot(a, b):
    return jnp.dot(a, b, preferred_element_type=F32)


def _dot_nt(a, b):
    return lax.dot_general(a, b, (((1,), (1,)), ((), ())), preferred_element_type=F32)


def _dot_tn(a, b):
    return lax.dot_general(a, b, (((0,), (0,)), ((), ())), preferred_element_type=F32)


def _rope_tables(pos_col):
    lane = lax.broadcasted_iota(jnp.int32, (1, LANES), 1)
    d = lane % ATT_HEAD_DIM
    half = ROT_HALF
    fidx = d % half
    inv_freq = jnp.zeros((1, LANES), F32)
    for i in range(half):
        inv_freq = jnp.where(fidx == i, 1.0 / (ROPE_THETA ** (2.0 * i / ROT_DIM)), inv_freq)
    ang = pos_col.astype(F32) * inv_freq
    cos = jnp.cos(ang)
    sin = jnp.sin(ang)
    lo = d < half
    hi = (d >= half) & (d < ROT_DIM)
    c = jnp.where(d < ROT_DIM, cos, 1.0)
    s_lo = jnp.where(lo, -sin, 0.0)
    s_hi = jnp.where(hi, sin, 0.0)
    return c, s_lo, s_hi


def _rope_apply(t, c, s_lo, s_hi):
    up = pltpu.roll(t, LANES - ROT_HALF, axis=1)
    dn = pltpu.roll(t, ROT_HALF, axis=1)
    return t * c + up * s_lo + dn * s_hi


def _lower_bound(lbp_ref):
    p = lbp_ref[...]
    m = jnp.max(p, axis=0, keepdims=True)
    e = jnp.exp(p - m)
    return e[0:1, :] / jnp.sum(e, axis=0, keepdims=True)


def _hgrn_prepare(q_r, fz, i_r, lb, scratch):
    q_s, lf_s, k_s, v_s = scratch[:4]
    q_s[...] = (q_r * _sigmoid(q_r)) * (HG_DIM ** -0.5)
    f = lb + (1.0 - lb) * _sigmoid(fz)
    k_s[...] = 1.0 - f
    lf_s[...] = jnp.log(f)
    v_s[...] = i_r.astype(BF16)


HG_GROUPS = HG_CHUNK // SUBLANES_F32
N_CHUNKS = SEQ_TILE // HG_CHUNK


def _hgrn_scratch():
    wide_bf = pltpu.VMEM((SEQ_TILE, HG_W), BF16)
    return [
        pltpu.VMEM((SEQ_TILE, HG_W), F32),
        pltpu.VMEM((SEQ_TILE, HG_W), F32),
        pltpu.VMEM((SEQ_TILE, HG_W), F32),
        wide_bf,
        wide_bf, wide_bf, wide_bf, wide_bf,
        pltpu.VMEM((N_CHUNKS, SUBLANES_F32, HG_W), F32),
        pltpu.VMEM((N_CHUNKS * HG_HEADS, HG_CHUNK, HG_CHUNK), BF16),
        pltpu.VMEM((N_CHUNKS * HG_HEADS, HG_DIM, HG_DIM), F32),
        pltpu.VMEM((N_CHUNKS * HG_HEADS, HG_DIM, HG_DIM), BF16),
        pltpu.VMEM((HG_HEADS, HG_DIM, HG_DIM), F32),
    ]


def _hgrn_scan(scratch, o_ref, reverse):
    q_s, cs_s, k_s, v_s, qin_s, kin_s, qst_s, kst_s, dec_s, a_s, u_s, sall_s, st_s = scratch
    end_row = 0 if reverse else HG_CHUNK - 1
    order = range(N_CHUNKS - 1, -1, -1) if reverse else range(N_CHUNKS)

    srow = lax.broadcasted_iota(jnp.int32, (SUBLANES_F32, HG_W), 0)
    for c in range(N_CHUNKS):
        carry = None
        for j in (range(HG_GROUPS - 1, -1, -1) if reverse else range(HG_GROUPS)):
            r0 = c * HG_CHUNK + j * SUBLANES_F32
            x = cs_s[r0:r0 + SUBLANES_F32, :]
            for s in (1, 2, 4):
                if reverse:
                    x = x + jnp.where(srow < SUBLANES_F32 - s,
                                      pltpu.roll(x, SUBLANES_F32 - s, axis=0), 0.0)
                else:
                    x = x + jnp.where(srow >= s, pltpu.roll(x, s, axis=0), 0.0)
            if carry is not None:
                x = x + carry
            cs_s[r0:r0 + SUBLANES_F32, :] = x
            edge = x[0:1, :] if reverse else x[SUBLANES_F32 - 1:SUBLANES_F32, :]
            carry = jnp.broadcast_to(edge, (SUBLANES_F32, HG_W))

    for c in range(N_CHUNKS):
        rows = slice(c * HG_CHUNK, (c + 1) * HG_CHUNK)
        cs = cs_s[rows, :]
        c_end = cs[end_row:end_row + 1, :]
        c_mid = cs[HG_MID:HG_MID + 1, :]
        q = q_s[rows, :]
        k = k_s[rows, :]
        qin_s[rows, :] = (q * jnp.exp(cs - c_mid)).astype(BF16)
        kin_s[rows, :] = (k * jnp.exp(c_mid - cs)).astype(BF16)
        qst_s[rows, :] = (q * jnp.exp(cs)).astype(BF16)
        kst_s[rows, :] = (k * jnp.exp(c_end - cs)).astype(BF16)
        dec_s[c] = jnp.broadcast_to(jnp.exp(c_end), (SUBLANES_F32, HG_W))

    row = lax.broadcasted_iota(jnp.int32, (HG_CHUNK, HG_CHUNK), 0)
    col = lax.broadcasted_iota(jnp.int32, (HG_CHUNK, HG_CHUNK), 1)
    keep = (col >= row) if reverse else (col <= row)
    for c in range(N_CHUNKS):
        rows = slice(c * HG_CHUNK, (c + 1) * HG_CHUNK)
        for h in range(HG_HEADS):
            sl = slice(h * HG_DIM, (h + 1) * HG_DIM)
            a = _dot_nt(qin_s[rows, sl], kin_s[rows, sl])
            a_s[c * HG_HEADS + h] = jnp.where(keep, a, 0.0).astype(BF16)
            u_s[c * HG_HEADS + h] = _dot_tn(v_s[rows, sl], kst_s[rows, sl])

    for h in range(HG_HEADS):
        sl = slice(h * HG_DIM, (h + 1) * HG_DIM)
        st = st_s[h]
        for c in order:
            sall_s[c * HG_HEADS + h] = st.astype(BF16)
            st = st * dec_s[c, 0:1, sl] + u_s[c * HG_HEADS + h]
        st_s[h] = st

    for c in range(N_CHUNKS):
        rows = slice(c * HG_CHUNK, (c + 1) * HG_CHUNK)
        for h in range(HG_HEADS):
            sl = slice(h * HG_DIM, (h + 1) * HG_DIM)
            i = c * HG_HEADS + h
            o_ref[rows, sl] = _dot(a_s[i], v_s[rows, sl]) + _dot_nt(qst_s[rows, sl], sall_s[i])


def _mem_kv_kernel(mem_ref, g_ref, w_ref, k_ref, v_ref):
    n = _rms(mem_ref[0], g_ref[...]).astype(BF16)
    kv = _dot(n, w_ref[...])
    k_ref[0] = kv[:, :D_MODEL].astype(BF16)
    v_ref[0] = kv[:, D_MODEL:].astype(BF16)


def _mem_kv(mem, g, w_kv):
    b = mem.shape[0]
    blk = pl.BlockSpec((1, MEM_LEN, D_MODEL), lambda i: (i, 0, 0))
    return pl.pallas_call(
        _mem_kv_kernel,
        grid=(b,),
        in_specs=[blk, _const_spec((1, D_MODEL)), _const_spec((D_MODEL, 2 * D_MODEL))],
        out_specs=[blk, blk],
        out_shape=[jax.ShapeDtypeStruct((b, MEM_LEN, D_MODEL), BF16)] * 2,
        compiler_params=pltpu.CompilerParams(
            dimension_semantics=("arbitrary",), vmem_limit_bytes=VMEM_LIMIT_BYTES),
        name="mem_kv",
    )(mem, g, w_kv)


BWD_COLS = 3 * HG_W + 2 * ATT_KVW


def _mix_bwd_kernel(x_ref, pos_ref, g_ref, w_ref, lbp_ref,
                    ob_ref, kk_ref, vv_ref,
                    *hg):
    st_s = hg[-1]

    @pl.when(pl.program_id(1) == 0)
    def _():
        st_s[...] = jnp.zeros_like(st_s)

    n = _rms(x_ref[0], g_ref[...]).astype(BF16)
    proj = _dot(n, w_ref[...])
    lb = _lower_bound(lbp_ref)
    _hgrn_prepare(proj[:, 0:HG_W], proj[:, HG_W:2 * HG_W], proj[:, 2 * HG_W:3 * HG_W], lb, hg)

    c, s_lo, s_hi = _rope_tables(pos_ref[0])
    k_a = _rope_apply(proj[:, 3 * HG_W:3 * HG_W + ATT_KVW], c, s_lo, s_hi)
    v_a = proj[:, 3 * HG_W + ATT_KVW:3 * HG_W + 2 * ATT_KVW]
    lane = lax.broadcasted_iota(jnp.int32, (1, LANES), 1)
    first = lane < ATT_HEAD_DIM
    for src, dst in ((k_a, kk_ref), (v_a, vv_ref)):
        sw = pltpu.roll(src, ATT_HEAD_DIM, axis=1)
        dst[0, :, 0 * LANES:1 * LANES] = jnp.where(first, src, 0.0).astype(BF16)
        dst[0, :, 1 * LANES:2 * LANES] = jnp.where(first, 0.0, sw).astype(BF16)
        dst[0, :, 2 * LANES:3 * LANES] = jnp.where(first, sw, 0.0).astype(BF16)
        dst[0, :, 3 * LANES:4 * LANES] = jnp.where(first, 0.0, src).astype(BF16)

    _hgrn_scan(hg, ob_ref.at[0], reverse=True)


def _mix_bwd(x, pos3, g, w, lbp):
    b, s, _ = x.shape
    nt = s // SEQ_TILE
    rev = lambda bi, i: (bi, nt - 1 - i, 0)
    return pl.pallas_call(
        _mix_bwd_kernel,
        grid=(b, nt),
        in_specs=[
            pl.BlockSpec((1, SEQ_TILE, D_MODEL), rev),
            pl.BlockSpec((1, SEQ_TILE, 1), rev),
            _const_spec((1, D_MODEL)),
            _const_spec((D_MODEL, BWD_COLS)),
            _const_spec(lbp.shape),
        ],
        out_specs=[
            pl.BlockSpec((1, SEQ_TILE, HG_W), rev),
            pl.BlockSpec((1, SEQ_TILE, 4 * LANES), rev),
            pl.BlockSpec((1, SEQ_TILE, 4 * LANES), rev),
        ],
        out_shape=[
            jax.ShapeDtypeStruct((b, s, HG_W), F32),
            jax.ShapeDtypeStruct((b, s, 4 * LANES), BF16),
            jax.ShapeDtypeStruct((b, s, 4 * LANES), BF16),
        ],
        scratch_shapes=_hgrn_scratch(),
        compiler_params=pltpu.CompilerParams(
            dimension_semantics=("arbitrary", "arbitrary"), vmem_limit_bytes=VMEM_LIMIT_BYTES),
        name="mix_bwd",
    )(x, pos3, g, w, lbp)


FWD_COLS = 4 * HG_W + ATT_QW + 2 * D_MODEL
Q_BLOCKS = SEQ_TILE // ATT_BLOCK
KEY_SPAN = 3 * ATT_BLOCK


def _mix_fwd_kernel(sink_ref, x_ref, pos_ref, ob_ref,
                    kk_prev, kk_main, kk_next, vv_prev, vv_main, vv_next,
                    g_ref, w_ref, lbp_ref, hgn_ref, wrec_ref, watt_ref, wout_ref,
                    h_ref,
                    proj_s, of_s, qrot_s, att_s, *hg):
    ti = pl.program_id(1)
    st_s = hg[-1]

    @pl.when(ti == 0)
    def _():
        st_s[...] = jnp.zeros_like(st_s)

    x = x_ref[0]
    n = _rms(x, g_ref[...]).astype(BF16)
    proj_s[...] = _dot(n, w_ref[...])

    lb = _lower_bound(lbp_ref)
    _hgrn_prepare(proj_s[:, 0:HG_W], proj_s[:, HG_W:2 * HG_W], proj_s[:, 2 * HG_W:3 * HG_W], lb, hg)
    _hgrn_scan(hg, of_s, reverse=False)
    o = of_s[...] + ob_ref[0]
    g_r = proj_s[:, 3 * HG_W:4 * HG_W]
    gate = g_r * _sigmoid(g_r)
    hgn = hgn_ref[...]
    parts = []
    for h in range(HG_HEADS):
        sl = slice(h * HG_DIM, (h + 1) * HG_DIM)
        oh = o[:, sl]
        ms = jnp.mean(oh * oh, axis=-1, keepdims=True)
        parts.append((oh * lax.rsqrt(ms + NORM_EPS) * hgn[:, sl] * gate[:, sl]).astype(BF16))
    y_rec = _dot(jnp.concatenate(parts, axis=1), wrec_ref[...])

    c, s_lo, s_hi = _rope_tables(pos_ref[0])
    q0 = 4 * HG_W
    for j in range(ATT_QW // LANES):
        t = proj_s[:, q0 + j * LANES:q0 + (j + 1) * LANES]
        qrot_s[:, j * LANES:(j + 1) * LANES] = (
            _rope_apply(t, c, s_lo, s_hi) * (ATT_HEAD_DIM ** -0.5)).astype(BF16)

    kk_all = jnp.concatenate([kk_prev[0], kk_main[0], kk_next[0]], axis=0)
    vv_all = jnp.concatenate([vv_prev[0], vv_main[0], vv_next[0]], axis=0)
    r_i = lax.broadcasted_iota(jnp.int32, (ATT_BLOCK, KEY_SPAN), 0)
    c_i = lax.broadcasted_iota(jnp.int32, (ATT_BLOCK, KEY_SPAN), 1)
    off = c_i - r_i
    lane = lax.broadcasted_iota(jnp.int32, (1, LANES), 1)
    first = lane < ATT_HEAD_DIM
    seq_len = pl.num_programs(1) * SEQ_TILE
    for j in range(Q_BLOCKS):
        kstart = ti * SEQ_TILE + (j - 1) * ATT_BLOCK
        kpos = c_i + kstart
        valid = (off >= 0) & (off <= 2 * WINDOW) & (kpos >= 0) & (kpos < seq_len)
        bias = jnp.where(valid, 0.0, -jnp.inf)
        rows = slice(j * ATT_BLOCK, (j + 1) * ATT_BLOCK)
        keys = slice(j * ATT_BLOCK, j * ATT_BLOCK + KEY_SPAN)
        for g in range(ATT_KV_HEADS):
            ca = slice(2 * g * LANES, (2 * g + 1) * LANES)
            cb = slice((2 * g + 1) * LANES, (2 * g + 2) * LANES)
            lhs = jnp.concatenate([qrot_s[rows, ca], qrot_s[rows, cb]], axis=0)
            k_bd = jnp.concatenate([kk_all[keys, ca], kk_all[keys, cb]], axis=0)
            v_bd = jnp.concatenate([vv_all[keys, ca], vv_all[keys, cb]], axis=0)
            s = _dot_nt(lhs, k_bd)
            p_rows, inv_rows = [], []
            for half in range(2):
                p_seg, inv_seg = [], []
                for seg in range(2):
                    head = 4 * g + 2 * half + seg
                    ss = s[half * ATT_BLOCK:(half + 1) * ATT_BLOCK,
                           seg * KEY_SPAN:(seg + 1) * KEY_SPAN] + bias
                    sk = sink_ref[head]
                    m = jnp.maximum(jnp.max(ss, axis=-1, keepdims=True), sk)
                    p = jnp.exp(ss - m)
                    den = jnp.sum(p, axis=-1, keepdims=True) + jnp.exp(sk - m)
                    p_seg.append(p.astype(BF16))
                    inv_seg.append(1.0 / den)
                p_rows.append(jnp.concatenate(p_seg, axis=1))
                inv_rows.append(jnp.where(first, inv_seg[0], inv_seg[1]))
            pv = _dot(jnp.concatenate(p_rows, axis=0), v_bd)
            att_s[rows, ca] = (pv[:ATT_BLOCK] * inv_rows[0]).astype(BF16)
            att_s[rows, cb] = (pv[ATT_BLOCK:] * inv_rows[1]).astype(BF16)
    y_att = _dot(att_s[...], watt_ref[...])

    g0 = 4 * HG_W + ATT_QW
    merged = (_sigmoid(proj_s[:, g0:g0 + D_MODEL]) * y_rec
              + _sigmoid(proj_s[:, g0 + D_MODEL:g0 + 2 * D_MODEL]) * y_att)
    h_ref[0] = x + _dot(merged.astype(BF16), wout_ref[...])


def _mix_fwd(sink, x, pos3, ob, kk, vv, g, w, lbp, hgn, wrec, watt, wout):
    b, s, _ = x.shape
    nt = s // SEQ_TILE
    r = SEQ_TILE // ATT_BLOCK
    nblk = s // ATT_BLOCK
    main = lambda bi, i: (bi, i, 0)
    prev = lambda bi, i: (bi, jnp.maximum(i * r - 1, 0), 0)
    nxt = lambda bi, i: (bi, jnp.minimum((i + 1) * r, nblk - 1), 0)
    halo = lambda imap: pl.BlockSpec((1, ATT_BLOCK, 4 * LANES), imap)
    tile = lambda w_: pl.BlockSpec((1, SEQ_TILE, w_), main)
    return pl.pallas_call(
        _mix_fwd_kernel,
        grid=(b, nt),
        in_specs=[
            pl.BlockSpec(memory_space=pltpu.SMEM),
            tile(D_MODEL), tile(1), tile(HG_W),
            halo(prev), tile(4 * LANES), halo(nxt),
            halo(prev), tile(4 * LANES), halo(nxt),
            _const_spec((1, D_MODEL)),
            _const_spec((D_MODEL, FWD_COLS)),
            _const_spec(lbp.shape),
            _const_spec((1, HG_W)),
            _const_spec((HG_W, D_MODEL)),
            _const_spec((ATT_QW, D_MODEL)),
            _const_spec((D_MODEL, D_MODEL)),
        ],
        out_specs=tile(D_MODEL),
        out_shape=jax.ShapeDtypeStruct((b, s, D_MODEL), F32),
        scratch_shapes=[
            pltpu.VMEM((SEQ_TILE, FWD_COLS), F32),
            pltpu.VMEM((SEQ_TILE, HG_W), F32),
            pltpu.VMEM((SEQ_TILE, ATT_QW), BF16),
            pltpu.VMEM((SEQ_TILE, ATT_QW), BF16),
        ] + _hgrn_scratch(),
        compiler_params=pltpu.CompilerParams(
            dimension_semantics=("arbitrary", "arbitrary"), vmem_limit_bytes=VMEM_LIMIT_BYTES),
        name="mix_fwd",
    )(sink, x, pos3, ob, kk, kk, kk, vv, vv, vv, g, w, lbp, hgn, wrec, watt, wout)


def _mem_xattn_kernel(h_ref, k_ref, v_ref, g_ref, wq_ref, wo_ref, o_ref, att_s):
    h = h_ref[0]
    n = _rms(h, g_ref[...]).astype(BF16)
    q = (_dot(n, wq_ref[...]) * (MEM_HEAD_DIM ** -0.5)).astype(BF16)
    for hd in range(MEM_HEADS):
        sl = slice(hd * MEM_HEAD_DIM, (hd + 1) * MEM_HEAD_DIM)
        s = _dot_nt(q[:, sl], k_ref[0, :, sl])
        m = jnp.max(s, axis=-1, keepdims=True)
        p = jnp.exp(s - m)
        inv = 1.0 / jnp.sum(p, axis=-1, keepdims=True)
        att_s[:, sl] = (_dot(p.astype(BF16), v_ref[0, :, sl]) * inv).astype(BF16)
    o_ref[0] = h + _dot(att_s[...], wo_ref[...])


def _mem_xattn(h, k_mem, v_mem, g, wq, wo):
    b, s, _ = h.shape
    tile = pl.BlockSpec((1, SEQ_TILE, D_MODEL), lambda bi, i: (bi, i, 0))
    kv = pl.BlockSpec((1, MEM_LEN, D_MODEL), lambda bi, i: (bi, 0, 0))
    return pl.pallas_call(
        _mem_xattn_kernel,
        grid=(b, s // SEQ_TILE),
        in_specs=[tile, kv, kv, _const_spec((1, D_MODEL)),
                  _const_spec((D_MODEL, D_MODEL)), _const_spec((D_MODEL, D_MODEL))],
        out_specs=tile,
        out_shape=jax.ShapeDtypeStruct((b, s, D_MODEL), F32),
        scratch_shapes=[pltpu.VMEM((SEQ_TILE, D_MODEL), BF16)],
        compiler_params=pltpu.CompilerParams(
            dimension_semantics=("arbitrary", "arbitrary"), vmem_limit_bytes=VMEM_LIMIT_BYTES),
        name="mem_xattn",
    )(h, k_mem, v_mem, g, wq, wo)


FFN_ROWS = SEQ_TILE + 2 * SUBLANES_F32


def _conv_ffn_kernel(hp_ref, h_ref, hn_ref, g_ref, wu_ref, wg_ref, cw_ref, cb_ref, wd_ref, fn_ref,
                     o_ref):
    ti = pl.program_id(1)
    h = h_ref[0]
    hcat = jnp.concatenate([hp_ref[0], h, hn_ref[0]], axis=0)
    nf = _rms(hcat, g_ref[...])
    n_all = nf.astype(BF16)
    n_main = nf[SUBLANES_F32:SUBLANES_F32 + SEQ_TILE].astype(BF16)
    ridx = lax.broadcasted_iota(jnp.int32, (FFN_ROWS, 1), 0)
    pad_lo = (ridx < SUBLANES_F32) & (ti == 0)
    pad_hi = (ridx >= SUBLANES_F32 + SEQ_TILE) & (ti == pl.num_programs(1) - 1)
    rowmask = jnp.where(pad_lo | pad_hi, 0.0, 1.0)
    main = slice(SUBLANES_F32, SUBLANES_F32 + SEQ_TILE)
    acc = jnp.zeros((SEQ_TILE, D_MODEL), F32)
    c0 = 0
    for width in FF_CHUNKS:
        cols = slice(c0, c0 + width)
        u = _dot(n_main, wu_ref[:, cols])
        gx = _dot(n_all, wg_ref[:, cols]) * rowmask
        g_prev = pltpu.roll(gx, 1, axis=0)[main]
        g_next = pltpu.roll(gx, FFN_ROWS - 1, axis=0)[main]
        gc = (g_prev * cw_ref[0:1, cols] + gx[main] * cw_ref[1:2, cols]
              + g_next * cw_ref[2:3, cols] + cb_ref[:, cols])
        a = (gc * _sigmoid(gc) * u).astype(BF16)
        acc = acc + _dot(a, wd_ref[cols, :])
        c0 += width
    o_ref[0] = _rms(h + acc, fn_ref[...])


def _conv_ffn(h, g, wu, wg, cw, cb, wd, fn):
    b, s, _ = h.shape
    r = SEQ_TILE // SUBLANES_F32
    nblk = s // SUBLANES_F32
    tile = pl.BlockSpec((1, SEQ_TILE, D_MODEL), lambda bi, i: (bi, i, 0))
    prev = pl.BlockSpec((1, SUBLANES_F32, D_MODEL), lambda bi, i: (bi, jnp.maximum(i * r - 1, 0), 0))
    nxt = pl.BlockSpec((1, SUBLANES_F32, D_MODEL),
                       lambda bi, i: (bi, jnp.minimum((i + 1) * r, nblk - 1), 0))
    return pl.pallas_call(
        _conv_ffn_kernel,
        grid=(b, s // SEQ_TILE),
        in_specs=[prev, tile, nxt, _const_spec((1, D_MODEL)),
                  _const_spec((D_MODEL, D_FF)), _const_spec((D_MODEL, D_FF)),
                  _const_spec((3, D_FF)), _const_spec((1, D_FF)),
                  _const_spec((D_FF, D_MODEL)), _const_spec((1, D_MODEL))],
        out_specs=tile,
        out_shape=jax.ShapeDtypeStruct((b, s, D_MODEL), F32),
        compiler_params=pltpu.CompilerParams(
            dimension_semantics=("arbitrary", "arbitrary"), vmem_limit_bytes=VMEM_LIMIT_BYTES),
        name="conv_ffn",
    )(h, h, h, g, wu, wg, cw, cb, wd, fn)


def kernel(x, mem, positions, norm_mix, w_in, lower_bounds, hg_norm, attn_sink, w_br_rec, w_br_att,
           w_mix_out, norm_mem, norm_mem_kv, w_mem_q, w_mem_kv, w_mem_o, norm_ffn, w_up, conv_w,
           conv_b, w_down, final_norm):
    assert x.shape[1] % SEQ_TILE == 0 and x.shape[2] == D_MODEL and w_in.shape[0] == 1
    bf = lambda a: a.astype(BF16)
    row = lambda a: a.reshape(1, -1).astype(F32)

    o = 0
    cols = {}
    for name, width in (("q_r", HG_W), ("fz_f", HG_W), ("fz_b", HG_W), ("i_r", HG_W), ("g_r", HG_W),
                        ("q_a", ATT_QW), ("k_a", ATT_KVW), ("v_a", ATT_KVW),
                        ("gate_r", D_MODEL), ("gate_a", D_MODEL)):
        cols[name] = w_in[0][:, o:o + width]
        o += width
    w_bwd = bf(jnp.concatenate([cols[k] for k in ("q_r", "fz_b", "i_r", "k_a", "v_a")], axis=1))
    w_fwd = bf(jnp.concatenate(
        [cols[k] for k in ("q_r", "fz_f", "i_r", "g_r", "q_a", "gate_r", "gate_a")], axis=1))

    pos3 = positions.reshape(positions.shape[0], positions.shape[1], 1)
    g_mix = row(norm_mix[0])

    k_mem, v_mem = _mem_kv(mem, row(norm_mem_kv[0]), bf(w_mem_kv[0]))
    o_bwd, kk, vv = _mix_bwd(x, pos3, g_mix, w_bwd, lower_bounds[1].astype(F32))
    h1 = _mix_fwd(attn_sink[0].astype(F32), x, pos3, o_bwd, kk, vv, g_mix, w_fwd,
                  lower_bounds[0].astype(F32), row(hg_norm[0]), bf(w_br_rec[0]), bf(w_br_att[0]),
                  bf(w_mix_out[0]))
    h2 = _mem_xattn(h1, k_mem, v_mem, row(norm_mem[0]), bf(w_mem_q[0]), bf(w_mem_o[0]))
    return _conv_ffn(h2, row(norm_ffn[0]), bf(w_up[0][:, :D_FF]), bf(w_up[0][:, D_FF:]),
                     conv_w[0].astype(F32), row(conv_b[0]), bf(w_down[0]), row(final_norm))
```

```python
import functools

import jax
import jax.numpy as jnp
from jax import lax
from jax.experimental import pallas as pl
from jax.experimental.pallas import tpu as pltpu

F32 = jnp.float32
BF16 = jnp.bfloat16

D_MODEL = 1024
MEM_LEN = 256
HG_HEADS = 4
HG_DIM = 128
HG_W = HG_HEADS * HG_DIM
HG_CHUNK = 64
HG_MID = HG_CHUNK // 2
ATT_Q_HEADS = 8
ATT_KV_HEADS = 2
ATT_HEAD_DIM = 64
ATT_QW = ATT_Q_HEADS * ATT_HEAD_DIM
ATT_KVW = ATT_KV_HEADS * ATT_HEAD_DIM
WINDOW = 128
ATT_BLOCK = 128
ROPE_THETA = 500000.0
ROT_DIM = ATT_HEAD_DIM // 4
ROT_HALF = ROT_DIM // 2
MEM_HEADS = 4
MEM_HEAD_DIM = D_MODEL // MEM_HEADS
D_FF = 2816
NORM_EPS = 1e-6

LANES = 128
SUBLANES_F32 = 8
VMEM_LIMIT_BYTES = 56 * 1024 * 1024

SEQ_TILE = 512
FF_CHUNKS = (768, 768, 768, 512)


def _const_spec(shape):
    nd = len(shape)
    return pl.BlockSpec(shape, lambda *_: (0,) * nd, pipeline_mode=pl.Buffered(1))


def _rms(x, g):
    ms = jnp.mean(x * x, axis=-1, keepdims=True)
    return x * lax.rsqrt(ms + NORM_EPS) * g


def _sigmoid(x):
    return 0.5 * jnp.tanh(0.5 * x) + 0.5


def _dot(a, b):
    return jnp.dot(a, b, preferred_element_type=F32)


def _dot_nt(a, b):
    return lax.dot_general(a, b, (((1,), (1,)), ((), ())), preferred_element_type=F32)


def _dot_tn(a, b):
    return lax.dot_general(a, b, (((0,), (0,)), ((), ())), preferred_element_type=F32)


POS_PER_ROW = LANES // ROT_HALF


def _rope_kernel(pos_ref, cos_ref, sin_ref):
    lane = lax.broadcasted_iota(jnp.int32, (1, LANES), 1)
    fidx = lane % ROT_HALF
    inv_freq = jnp.zeros((1, LANES), F32)
    for i in range(ROT_HALF):
        inv_freq = jnp.where(fidx == i, 1.0 / (ROPE_THETA ** (2.0 * i / ROT_DIM)), inv_freq)
    ang = pos_ref[...].astype(F32) * inv_freq
    cos_ref[...] = jnp.cos(ang)
    sin_ref[...] = jnp.sin(ang)


def _rope_tables(positions):
    b, s = positions.shape
    pos_rep = jnp.repeat(positions.reshape(-1, POS_PER_ROW), ROT_HALF, axis=1)
    full = pl.BlockSpec(pos_rep.shape, lambda: (0, 0))
    cos, sin = pl.pallas_call(
        _rope_kernel,
        in_specs=[full],
        out_specs=[full, full],
        out_shape=[jax.ShapeDtypeStruct(pos_rep.shape, F32)] * 2,
        name="rope_tab",
    )(pos_rep)
    cos = cos.reshape(b, s, ROT_HALF)
    sin = sin.reshape(b, s, ROT_HALF)
    rest = ATT_HEAD_DIM - ROT_DIM
    zeros = lambda w: jnp.zeros((b, s, w), F32)
    head_c = jnp.concatenate([cos, cos, jnp.ones((b, s, rest), F32)], axis=-1)
    head_lo = jnp.concatenate([-sin, zeros(ROT_HALF + rest)], axis=-1)
    head_hi = jnp.concatenate([zeros(ROT_HALF), sin, zeros(rest)], axis=-1)
    return jnp.concatenate([head_c, head_c, head_lo, head_lo, head_hi, head_hi], axis=-1)


def _rope_apply(t, rope_ref):
    up = pltpu.roll(t, LANES - ROT_HALF, axis=1)
    dn = pltpu.roll(t, ROT_HALF, axis=1)
    return (t * rope_ref[0, :, 0:LANES] + up * rope_ref[0, :, LANES:2 * LANES]
            + dn * rope_ref[0, :, 2 * LANES:3 * LANES])


def _lower_bound(lbp_ref):
    p = lbp_ref[...]
    m = jnp.max(p, axis=0, keepdims=True)
    e = jnp.exp(p - m)
    return e[0:1, :] / jnp.sum(e, axis=0, keepdims=True)


def _hgrn_gates(fz, lb, scratch):
    _, lf_s, k_s, _ = scratch[:4]
    f = lb + (1.0 - lb) * _sigmoid(fz)
    k_s[...] = 1.0 - f
    lf_s[...] = jnp.log(f)


HG_GROUPS = HG_CHUNK // SUBLANES_F32
N_CHUNKS = SEQ_TILE // HG_CHUNK


def _hgrn_scratch():
    wide_bf = pltpu.VMEM((SEQ_TILE, HG_W), BF16)
    return [
        pltpu.VMEM((SEQ_TILE, HG_W), F32),
        pltpu.VMEM((SEQ_TILE, HG_W), F32),
        pltpu.VMEM((SEQ_TILE, HG_W), F32),
        wide_bf,
        wide_bf, wide_bf, wide_bf, wide_bf,
        pltpu.VMEM((N_CHUNKS, SUBLANES_F32, HG_W), F32),
        pltpu.VMEM((N_CHUNKS * HG_HEADS, HG_CHUNK, HG_CHUNK), BF16),
        pltpu.VMEM((N_CHUNKS * HG_HEADS, HG_DIM, HG_DIM), F32),
        pltpu.VMEM((N_CHUNKS * HG_HEADS, HG_DIM, HG_DIM), BF16),
        pltpu.VMEM((HG_HEADS, HG_DIM, HG_DIM), F32),
    ]


def _hgrn_scan(scratch, o_ref, reverse):
    q_s, cs_s, k_s, v_s, qin_s, kin_s, qst_s, kst_s, dec_s, a_s, u_s, sall_s, st_s = scratch
    end_row = 0 if reverse else HG_CHUNK - 1
    order = range(N_CHUNKS - 1, -1, -1) if reverse else range(N_CHUNKS)

    srow = lax.broadcasted_iota(jnp.int32, (SUBLANES_F32, HG_W), 0)
    for c in range(N_CHUNKS):
        carry = None
        for j in (range(HG_GROUPS - 1, -1, -1) if reverse else range(HG_GROUPS)):
            r0 = c * HG_CHUNK + j * SUBLANES_F32
            x = cs_s[r0:r0 + SUBLANES_F32, :]
            for s in (1, 2, 4):
                if reverse:
                    x = x + jnp.where(srow < SUBLANES_F32 - s,
                                      pltpu.roll(x, SUBLANES_F32 - s, axis=0), 0.0)
                else:
                    x = x + jnp.where(srow >= s, pltpu.roll(x, s, axis=0), 0.0)
            if carry is not None:
                x = x + carry
            cs_s[r0:r0 + SUBLANES_F32, :] = x
            edge = x[0:1, :] if reverse else x[SUBLANES_F32 - 1:SUBLANES_F32, :]
            carry = jnp.broadcast_to(edge, (SUBLANES_F32, HG_W))

    for c in range(N_CHUNKS):
        rows = slice(c * HG_CHUNK, (c + 1) * HG_CHUNK)
        cs = cs_s[rows, :]
        c_end = cs[end_row:end_row + 1, :]
        c_mid = cs[HG_MID:HG_MID + 1, :]
        q = q_s[rows, :]
        k = k_s[rows, :]
        qin_s[rows, :] = (q * jnp.exp(cs - c_mid)).astype(BF16)
        kin_s[rows, :] = (k * jnp.exp(c_mid - cs)).astype(BF16)
        qst_s[rows, :] = (q * jnp.exp(cs)).astype(BF16)
        kst_s[rows, :] = (k * jnp.exp(c_end - cs)).astype(BF16)
        dec_s[c] = jnp.broadcast_to(jnp.exp(c_end), (SUBLANES_F32, HG_W))

    row = lax.broadcasted_iota(jnp.int32, (HG_CHUNK, HG_CHUNK), 0)
    col = lax.broadcasted_iota(jnp.int32, (HG_CHUNK, HG_CHUNK), 1)
    keep = (col >= row) if reverse else (col <= row)
    for c in range(N_CHUNKS):
        rows = slice(c * HG_CHUNK, (c + 1) * HG_CHUNK)
        for h in range(HG_HEADS):
            sl = slice(h * HG_DIM, (h + 1) * HG_DIM)
            a = _dot_nt(qin_s[rows, sl], kin_s[rows, sl])
            a_s[c * HG_HEADS + h] = jnp.where(keep, a, 0.0).astype(BF16)
            u_s[c * HG_HEADS + h] = _dot_tn(v_s[rows, sl], kst_s[rows, sl])

    for h in range(HG_HEADS):
        sl = slice(h * HG_DIM, (h + 1) * HG_DIM)
        st = st_s[h]
        for c in order:
            sall_s[c * HG_HEADS + h] = st.astype(BF16)
            st = st * dec_s[c, 0:1, sl] + u_s[c * HG_HEADS + h]
        st_s[h] = st

    for c in range(N_CHUNKS):
        rows = slice(c * HG_CHUNK, (c + 1) * HG_CHUNK)
        for h in range(HG_HEADS):
            sl = slice(h * HG_DIM, (h + 1) * HG_DIM)
            i = c * HG_HEADS + h
            o_ref[rows, sl] = _dot(a_s[i], v_s[rows, sl]) + _dot_nt(qst_s[rows, sl], sall_s[i])


def _mem_kv_kernel(mem_ref, g_ref, w_ref, k_ref, v_ref):
    n = _rms(mem_ref[0], g_ref[...]).astype(BF16)
    kv = _dot(n, w_ref[...])
    k_ref[0] = kv[:, :D_MODEL].astype(BF16)
    v_ref[0] = kv[:, D_MODEL:].astype(BF16)


def _mem_kv(mem, g, w_kv):
    b = mem.shape[0]
    blk = pl.BlockSpec((1, MEM_LEN, D_MODEL), lambda i: (i, 0, 0))
    return pl.pallas_call(
        _mem_kv_kernel,
        grid=(b,),
        in_specs=[blk, _const_spec((1, D_MODEL)), _const_spec((D_MODEL, 2 * D_MODEL))],
        out_specs=[blk, blk],
        out_shape=[jax.ShapeDtypeStruct((b, MEM_LEN, D_MODEL), BF16)] * 2,
        compiler_params=pltpu.CompilerParams(
            dimension_semantics=("arbitrary",), vmem_limit_bytes=VMEM_LIMIT_BYTES),
        name="mem_kv",
    )(mem, g, w_kv)


BWD_COLS = 3 * HG_W + 2 * ATT_KVW


def _mix_bwd_kernel(x_ref, rope_ref, g_ref, w_ref, lbp_ref,
                    ob_ref, kk_ref, vv_ref, qh_ref, vh_ref,
                    *hg):
    q_s, _, _, v_s = hg[:4]
    st_s = hg[-1]

    @pl.when(pl.program_id(1) == 0)
    def _():
        st_s[...] = jnp.zeros_like(st_s)

    n = _rms(x_ref[0], g_ref[...]).astype(BF16)
    proj = _dot(n, w_ref[...])
    q_r = proj[:, 0:HG_W]
    q = (q_r * _sigmoid(q_r)) * (HG_DIM ** -0.5)
    v = proj[:, 2 * HG_W:3 * HG_W].astype(BF16)
    q_s[...] = q
    v_s[...] = v
    qh_ref[0] = q.astype(BF16)
    vh_ref[0] = v
    _hgrn_gates(proj[:, HG_W:2 * HG_W], _lower_bound(lbp_ref), hg)

    k_a = _rope_apply(proj[:, 3 * HG_W:3 * HG_W + ATT_KVW], rope_ref)
    v_a = proj[:, 3 * HG_W + ATT_KVW:3 * HG_W + 2 * ATT_KVW]
    lane = lax.broadcasted_iota(jnp.int32, (1, LANES), 1)
    first = lane < ATT_HEAD_DIM
    for src, dst in ((k_a, kk_ref), (v_a, vv_ref)):
        sw = pltpu.roll(src, ATT_HEAD_DIM, axis=1)
        dst[0, :, 0 * LANES:1 * LANES] = jnp.where(first, src, 0.0).astype(BF16)
        dst[0, :, 1 * LANES:2 * LANES] = jnp.where(first, 0.0, sw).astype(BF16)
        dst[0, :, 2 * LANES:3 * LANES] = jnp.where(first, sw, 0.0).astype(BF16)
        dst[0, :, 3 * LANES:4 * LANES] = jnp.where(first, 0.0, src).astype(BF16)

    _hgrn_scan(hg, ob_ref.at[0], reverse=True)


def _mix_bwd(x, rope, g, w, lbp):
    b, s, _ = x.shape
    nt = s // SEQ_TILE
    rev = lambda bi, i: (bi, nt - 1 - i, 0)
    tile = lambda w_: pl.BlockSpec((1, SEQ_TILE, w_), rev)
    wide_bf = jax.ShapeDtypeStruct((b, s, HG_W), BF16)
    return pl.pallas_call(
        _mix_bwd_kernel,
        grid=(b, nt),
        in_specs=[
            tile(D_MODEL), tile(3 * LANES),
            _const_spec((1, D_MODEL)),
            _const_spec((D_MODEL, BWD_COLS)),
            _const_spec(lbp.shape),
        ],
        out_specs=[tile(HG_W), tile(4 * LANES), tile(4 * LANES), tile(HG_W), tile(HG_W)],
        out_shape=[jax.ShapeDtypeStruct((b, s, HG_W), F32), wide_bf, wide_bf, wide_bf, wide_bf],
        scratch_shapes=_hgrn_scratch(),
        compiler_params=pltpu.CompilerParams(
            dimension_semantics=("arbitrary", "arbitrary"), vmem_limit_bytes=VMEM_LIMIT_BYTES),
        name="mix_bwd",
    )(x, rope, g, w, lbp)


FWD_COLS = 2 * HG_W + ATT_QW + 2 * D_MODEL
Q_BLOCKS = SEQ_TILE // ATT_BLOCK
KEY_SPAN = 3 * ATT_BLOCK


def _mix_fwd_kernel(sink_ref, x_ref, rope_ref, ob_ref, qh_ref, vh_ref,
                    kk_prev, kk_main, kk_next, vv_prev, vv_main, vv_next,
                    g_ref, w_ref, lbp_ref, hgn_ref, wrec_ref, watt_ref, wout_ref,
                    h_ref,
                    proj_s, of_s, qrot_s, att_s, *hg):
    ti = pl.program_id(1)
    q_s, _, _, v_s = hg[:4]
    st_s = hg[-1]

    @pl.when(ti == 0)
    def _():
        st_s[...] = jnp.zeros_like(st_s)

    x = x_ref[0]
    n = _rms(x, g_ref[...]).astype(BF16)
    proj_s[...] = _dot(n, w_ref[...])

    q_s[...] = qh_ref[0].astype(F32)
    v_s[...] = vh_ref[0]
    _hgrn_gates(proj_s[:, 0:HG_W], _lower_bound(lbp_ref), hg)
    _hgrn_scan(hg, of_s, reverse=False)
    o = of_s[...] + ob_ref[0]
    g_r = proj_s[:, HG_W:2 * HG_W]
    gate = g_r * _sigmoid(g_r)
    hgn = hgn_ref[...]
    parts = []
    for h in range(HG_HEADS):
        sl = slice(h * HG_DIM, (h + 1) * HG_DIM)
        oh = o[:, sl]
        ms = jnp.mean(oh * oh, axis=-1, keepdims=True)
        parts.append((oh * lax.rsqrt(ms + NORM_EPS) * hgn[:, sl] * gate[:, sl]).astype(BF16))
    y_rec = _dot(jnp.concatenate(parts, axis=1), wrec_ref[...])

    q0 = 2 * HG_W
    for j in range(ATT_QW // LANES):
        t = proj_s[:, q0 + j * LANES:q0 + (j + 1) * LANES]
        qrot_s[:, j * LANES:(j + 1) * LANES] = (
            _rope_apply(t, rope_ref) * (ATT_HEAD_DIM ** -0.5)).astype(BF16)

    kk_all = jnp.concatenate([kk_prev[0], kk_main[0], kk_next[0]], axis=0)
    vv_all = jnp.concatenate([vv_prev[0], vv_main[0], vv_next[0]], axis=0)
    r_i = lax.broadcasted_iota(jnp.int32, (ATT_BLOCK, KEY_SPAN), 0)
    c_i = lax.broadcasted_iota(jnp.int32, (ATT_BLOCK, KEY_SPAN), 1)
    off = c_i - r_i
    lane = lax.broadcasted_iota(jnp.int32, (1, LANES), 1)
    first = lane < ATT_HEAD_DIM
    seq_len = pl.num_programs(1) * SEQ_TILE
    for j in range(Q_BLOCKS):
        kstart = ti * SEQ_TILE + (j - 1) * ATT_BLOCK
        kpos = c_i + kstart
        valid = (off >= 0) & (off <= 2 * WINDOW) & (kpos >= 0) & (kpos < seq_len)
        bias = jnp.where(valid, 0.0, -jnp.inf)
        rows = slice(j * ATT_BLOCK, (j + 1) * ATT_BLOCK)
        keys = slice(j * ATT_BLOCK, j * ATT_BLOCK + KEY_SPAN)
        for g in range(ATT_KV_HEADS):
            ca = slice(2 * g * LANES, (2 * g + 1) * LANES)
            cb = slice((2 * g + 1) * LANES, (2 * g + 2) * LANES)
            lhs = jnp.concatenate([qrot_s[rows, ca], qrot_s[rows, cb]], axis=0)
            k_bd = jnp.concatenate([kk_all[keys, ca], kk_all[keys, cb]], axis=0)
            v_bd = jnp.concatenate([vv_all[keys, ca], vv_all[keys, cb]], axis=0)
            s = _dot_nt(lhs, k_bd)
            p_rows, inv_rows = [], []
            for half in range(2):
                p_seg, inv_seg = [], []
                for seg in range(2):
                    head = 4 * g + 2 * half + seg
                    ss = s[half * ATT_BLOCK:(half + 1) * ATT_BLOCK,
                           seg * KEY_SPAN:(seg + 1) * KEY_SPAN] + bias
                    sk = sink_ref[head]
                    m = jnp.maximum(jnp.max(ss, axis=-1, keepdims=True), sk)
                    p = jnp.exp(ss - m)
                    den = jnp.sum(p, axis=-1, keepdims=True) + jnp.exp(sk - m)
                    p_seg.append(p.astype(BF16))
                    inv_seg.append(1.0 / den)
                p_rows.append(jnp.concatenate(p_seg, axis=1))
                inv_rows.append(jnp.where(first, inv_seg[0], inv_seg[1]))
            pv = _dot(jnp.concatenate(p_rows, axis=0), v_bd)
            att_s[rows, ca] = (pv[:ATT_BLOCK] * inv_rows[0]).astype(BF16)
            att_s[rows, cb] = (pv[ATT_BLOCK:] * inv_rows[1]).astype(BF16)
    y_att = _dot(att_s[...], watt_ref[...])

    g0 = 2 * HG_W + ATT_QW
    merged = (_sigmoid(proj_s[:, g0:g0 + D_MODEL]) * y_rec
              + _sigmoid(proj_s[:, g0 + D_MODEL:g0 + 2 * D_MODEL]) * y_att)
    h_ref[0] = x + _dot(merged.astype(BF16), wout_ref[...])


def _mix_fwd(sink, x, rope, ob, qh, vh, kk, vv, g, w, lbp, hgn, wrec, watt, wout):
    b, s, _ = x.shape
    nt = s // SEQ_TILE
    r = SEQ_TILE // ATT_BLOCK
    nblk = s // ATT_BLOCK
    main = lambda bi, i: (bi, i, 0)
    prev = lambda bi, i: (bi, jnp.maximum(i * r - 1, 0), 0)
    nxt = lambda bi, i: (bi, jnp.minimum((i + 1) * r, nblk - 1), 0)
    halo = lambda imap: pl.BlockSpec((1, ATT_BLOCK, 4 * LANES), imap)
    tile = lambda w_: pl.BlockSpec((1, SEQ_TILE, w_), main)
    return pl.pallas_call(
        _mix_fwd_kernel,
        grid=(b, nt),
        in_specs=[
            pl.BlockSpec(memory_space=pltpu.SMEM),
            tile(D_MODEL), tile(3 * LANES), tile(HG_W), tile(HG_W), tile(HG_W),
            halo(prev), tile(4 * LANES), halo(nxt),
            halo(prev), tile(4 * LANES), halo(nxt),
            _const_spec((1, D_MODEL)),
            _const_spec((D_MODEL, FWD_COLS)),
            _const_spec(lbp.shape),
            _const_spec((1, HG_W)),
            _const_spec((HG_W, D_MODEL)),
            _const_spec((ATT_QW, D_MODEL)),
            _const_spec((D_MODEL, D_MODEL)),
        ],
        out_specs=tile(D_MODEL),
        out_shape=jax.ShapeDtypeStruct((b, s, D_MODEL), F32),
        scratch_shapes=[
            pltpu.VMEM((SEQ_TILE, FWD_COLS), F32),
            pltpu.VMEM((SEQ_TILE, HG_W), F32),
            pltpu.VMEM((SEQ_TILE, ATT_QW), BF16),
            pltpu.VMEM((SEQ_TILE, ATT_QW), BF16),
        ] + _hgrn_scratch(),
        compiler_params=pltpu.CompilerParams(
            dimension_semantics=("arbitrary", "arbitrary"), vmem_limit_bytes=VMEM_LIMIT_BYTES),
        name="mix_fwd",
    )(sink, x, rope, ob, qh, vh, kk, kk, kk, vv, vv, vv, g, w, lbp, hgn, wrec, watt, wout)


def _mem_xattn_kernel(h_ref, k_ref, v_ref, g_ref, wq_ref, wo_ref, o_ref, att_s):
    h = h_ref[0]
    n = _rms(h, g_ref[...]).astype(BF16)
    q = (_dot(n, wq_ref[...]) * (MEM_HEAD_DIM ** -0.5)).astype(BF16)
    for hd in range(MEM_HEADS):
        sl = slice(hd * MEM_HEAD_DIM, (hd + 1) * MEM_HEAD_DIM)
        s = _dot_nt(q[:, sl], k_ref[0, :, sl])
        m = jnp.max(s, axis=-1, keepdims=True)
        p = jnp.exp(s - m)
        inv = 1.0 / jnp.sum(p, axis=-1, keepdims=True)
        att_s[:, sl] = (_dot(p.astype(BF16), v_ref[0, :, sl]) * inv).astype(BF16)
    o_ref[0] = h + _dot(att_s[...], wo_ref[...])


def _mem_xattn(h, k_mem, v_mem, g, wq, wo):
    b, s, _ = h.shape
    tile = pl.BlockSpec((1, SEQ_TILE, D_MODEL), lambda bi, i: (bi, i, 0))
    kv = pl.BlockSpec((1, MEM_LEN, D_MODEL), lambda bi, i: (bi, 0, 0))
    return pl.pallas_call(
        _mem_xattn_kernel,
        grid=(b, s // SEQ_TILE),
        in_specs=[tile, kv, kv, _const_spec((1, D_MODEL)),
                  _const_spec((D_MODEL, D_MODEL)), _const_spec((D_MODEL, D_MODEL))],
        out_specs=tile,
        out_shape=jax.ShapeDtypeStruct((b, s, D_MODEL), F32),
        scratch_shapes=[pltpu.VMEM((SEQ_TILE, D_MODEL), BF16)],
        compiler_params=pltpu.CompilerParams(
            dimension_semantics=("arbitrary", "arbitrary"), vmem_limit_bytes=VMEM_LIMIT_BYTES),
        name="mem_xattn",
    )(h, k_mem, v_mem, g, wq, wo)


FFN_ROWS = SEQ_TILE + 2 * SUBLANES_F32


def _conv_ffn_kernel(hp_ref, h_ref, hn_ref, g_ref, wu_ref, wg_ref, cw_ref, cb_ref, wd_ref, fn_ref,
                     o_ref):
    ti = pl.program_id(1)
    h = h_ref[0]
    hcat = jnp.concatenate([hp_ref[0], h, hn_ref[0]], axis=0)
    nf = _rms(hcat, g_ref[...])
    n_all = nf.astype(BF16)
    n_main = nf[SUBLANES_F32:SUBLANES_F32 + SEQ_TILE].astype(BF16)
    ridx = lax.broadcasted_iota(jnp.int32, (FFN_ROWS, 1), 0)
    pad_lo = (ridx < SUBLANES_F32) & (ti == 0)
    pad_hi = (ridx >= SUBLANES_F32 + SEQ_TILE) & (ti == pl.num_programs(1) - 1)
    rowmask = jnp.where(pad_lo | pad_hi, 0.0, 1.0)
    main = slice(SUBLANES_F32, SUBLANES_F32 + SEQ_TILE)
    acc = jnp.zeros((SEQ_TILE, D_MODEL), F32)
    c0 = 0
    for width in FF_CHUNKS:
        cols = slice(c0, c0 + width)
        u = _dot(n_main, wu_ref[:, cols])
        gx = _dot(n_all, wg_ref[:, cols]) * rowmask
        g_prev = pltpu.roll(gx, 1, axis=0)[main]
        g_next = pltpu.roll(gx, FFN_ROWS - 1, axis=0)[main]
        gc = (g_prev * cw_ref[0:1, cols] + gx[main] * cw_ref[1:2, cols]
              + g_next * cw_ref[2:3, cols] + cb_ref[:, cols])
        a = (gc * _sigmoid(gc) * u).astype(BF16)
        acc = acc + _dot(a, wd_ref[cols, :])
        c0 += width
    o_ref[0] = _rms(h + acc, fn_ref[...])


def _conv_ffn(h, g, wu, wg, cw, cb, wd, fn):
    b, s, _ = h.shape
    r = SEQ_TILE // SUBLANES_F32
    nblk = s // SUBLANES_F32
    tile = pl.BlockSpec((1, SEQ_TILE, D_MODEL), lambda bi, i: (bi, i, 0))
    prev = pl.BlockSpec((1, SUBLANES_F32, D_MODEL), lambda bi, i: (bi, jnp.maximum(i * r - 1, 0), 0))
    nxt = pl.BlockSpec((1, SUBLANES_F32, D_MODEL),
                       lambda bi, i: (bi, jnp.minimum((i + 1) * r, nblk - 1), 0))
    return pl.pallas_call(
        _conv_ffn_kernel,
        grid=(b, s // SEQ_TILE),
        in_specs=[prev, tile, nxt, _const_spec((1, D_MODEL)),
                  _const_spec((D_MODEL, D_FF)), _const_spec((D_MODEL, D_FF)),
                  _const_spec((3, D_FF)), _const_spec((1, D_FF)),
                  _const_spec((D_FF, D_MODEL)), _const_spec((1, D_MODEL))],
        out_specs=tile,
        out_shape=jax.ShapeDtypeStruct((b, s, D_MODEL), F32),
        compiler_params=pltpu.CompilerParams(
            dimension_semantics=("arbitrary", "arbitrary"), vmem_limit_bytes=VMEM_LIMIT_BYTES),
        name="conv_ffn",
    )(h, h, h, g, wu, wg, cw, cb, wd, fn)


def kernel(x, mem, positions, norm_mix, w_in, lower_bounds, hg_norm, attn_sink, w_br_rec, w_br_att,
           w_mix_out, norm_mem, norm_mem_kv, w_mem_q, w_mem_kv, w_mem_o, norm_ffn, w_up, conv_w,
           conv_b, w_down, final_norm):
    assert x.shape[1] % SEQ_TILE == 0 and x.shape[2] == D_MODEL and w_in.shape[0] == 1
    bf = lambda a: a.astype(BF16)
    row = lambda a: a.reshape(1, -1).astype(F32)

    o = 0
    cols = {}
    for name, width in (("q_r", HG_W), ("fz_f", HG_W), ("fz_b", HG_W), ("i_r", HG_W), ("g_r", HG_W),
                        ("q_a", ATT_QW), ("k_a", ATT_KVW), ("v_a", ATT_KVW),
                        ("gate_r", D_MODEL), ("gate_a", D_MODEL)):
        cols[name] = w_in[0][:, o:o + width]
        o += width
    w_bwd = bf(jnp.concatenate([cols[k] for k in ("q_r", "fz_b", "i_r", "k_a", "v_a")], axis=1))
    w_fwd = bf(jnp.concatenate(
        [cols[k] for k in ("fz_f", "g_r", "q_a", "gate_r", "gate_a")], axis=1))

    rope = _rope_tables(positions)
    g_mix = row(norm_mix[0])

    k_mem, v_mem = _mem_kv(mem, row(norm_mem_kv[0]), bf(w_mem_kv[0]))
    o_bwd, kk, vv, qh, vh = _mix_bwd(x, rope, g_mix, w_bwd, lower_bounds[1].astype(F32))
    h1 = _mix_fwd(attn_sink[0].astype(F32), x, rope, o_bwd, qh, vh, kk, vv, g_mix, w_fwd,
                  lower_bounds[0].astype(F32), row(hg_norm[0]), bf(w_br_rec[0]), bf(w_br_att[0]),
                  bf(w_mix_out[0]))
    h2 = _mem_xattn(h1, k_mem, v_mem, row(norm_mem[0]), bf(w_mem_q[0]), bf(w_mem_o[0]))
    return _conv_ffn(h2, row(norm_ffn[0]), bf(w_up[0][:, :D_FF]), bf(w_up[0][:, D_FF:]),
                     conv_w[0].astype(F32), row(conv_b[0]), bf(w_down[0]), row(final_norm))
```

```python
import functools

import jax
import jax.numpy as jnp
from jax import lax
from jax.experimental import pallas as pl
from jax.experimental.pallas import tpu as pltpu

F32 = jnp.float32
BF16 = jnp.bfloat16

D_MODEL = 1024
MEM_LEN = 256
HG_HEADS = 4
HG_DIM = 128
HG_W = HG_HEADS * HG_DIM
HG_CHUNK = 64
HG_MID = HG_CHUNK // 2
ATT_Q_HEADS = 8
ATT_KV_HEADS = 2
ATT_HEAD_DIM = 64
ATT_QW = ATT_Q_HEADS * ATT_HEAD_DIM
ATT_KVW = ATT_KV_HEADS * ATT_HEAD_DIM
WINDOW = 128
ATT_BLOCK = 128
ROPE_THETA = 500000.0
ROT_DIM = ATT_HEAD_DIM // 4
ROT_HALF = ROT_DIM // 2
MEM_HEADS = 4
MEM_HEAD_DIM = D_MODEL // MEM_HEADS
D_FF = 2816
NORM_EPS = 1e-6

LANES = 128
SUBLANES_F32 = 8
VMEM_LIMIT_BYTES = 56 * 1024 * 1024

SEQ_TILE = 512
FF_CHUNKS = (768, 768, 768, 512)


def _const_spec(shape):
    nd = len(shape)
    return pl.BlockSpec(shape, lambda *_: (0,) * nd, pipeline_mode=pl.Buffered(1))


def _rms(x, g):
    ms = jnp.mean(x * x, axis=-1, keepdims=True)
    return x * lax.rsqrt(ms + NORM_EPS) * g


def _sigmoid(x):
    return 0.5 * jnp.tanh(0.5 * x) + 0.5


def _dot(a, b):
    return jnp.dot(a, b, preferred_element_type=F32)


def _dot_nt(a, b):
    return lax.dot_general(a, b, (((1,), (1,)), ((), ())), preferred_element_type=F32)


def _dot_tn(a, b):
    return lax.dot_general(a, b, (((0,), (0,)), ((), ())), preferred_element_type=F32)


ROPE_SLOTS = LANES // ROT_DIM
ROPE_ROWS = SEQ_TILE // ROPE_SLOTS


def _rope_kernel(pos_ref, cos_ref, sin_ref):
    lane = lax.broadcasted_iota(jnp.int32, (1, LANES), 1)
    fidx = lane % ROT_HALF
    inv_freq = jnp.zeros((1, LANES), F32)
    for i in range(ROT_HALF):
        inv_freq = jnp.where(fidx == i, 1.0 / (ROPE_THETA ** (2.0 * i / ROT_DIM)), inv_freq)
    ang = pos_ref[...].astype(F32) * inv_freq
    cos_ref[...] = jnp.cos(ang)
    sin_ref[...] = jnp.sin(ang)


def _rope_tables(positions):
    b, s = positions.shape
    nt = s // SEQ_TILE
    p = positions.reshape(b, nt, ROPE_SLOTS, ROPE_ROWS).transpose(0, 1, 3, 2)
    p = jnp.broadcast_to(p[..., None], (b, nt, ROPE_ROWS, ROPE_SLOTS, ROT_DIM))
    p = p.reshape(b * nt * ROPE_ROWS, LANES)
    full = pl.BlockSpec(p.shape, lambda: (0, 0))
    cos, sin = pl.pallas_call(
        _rope_kernel,
        in_specs=[full],
        out_specs=[full, full],
        out_shape=[jax.ShapeDtypeStruct(p.shape, F32)] * 2,
        name="rope_tab",
    )(p)
    return cos.reshape(b, nt * ROPE_ROWS, LANES), sin.reshape(b, nt * ROPE_ROWS, LANES)


def _rope_block(cos_ref, sin_ref, slot):
    lane = lax.broadcasted_iota(jnp.int32, (1, LANES), 1)
    d = lane % ATT_HEAD_DIM
    first = lane < ATT_HEAD_DIM

    def spread(packed):
        sh = (LANES - slot * ROT_DIM) % LANES
        a = pltpu.roll(packed, sh, axis=1) if sh else packed
        b = pltpu.roll(packed, (sh + ATT_HEAD_DIM) % LANES, axis=1)
        return jnp.where(first, a, b)

    cos = spread(cos_ref[0])
    sin = spread(sin_ref[0])
    c = jnp.where(d < ROT_DIM, cos, 1.0)
    s_lo = jnp.where(d < ROT_HALF, -sin, 0.0)
    s_hi = jnp.where((d >= ROT_HALF) & (d < ROT_DIM), sin, 0.0)
    return c, s_lo, s_hi


def _rope_apply(t, c, s_lo, s_hi):
    up = pltpu.roll(t, LANES - ROT_HALF, axis=1)
    dn = pltpu.roll(t, ROT_HALF, axis=1)
    return t * c + up * s_lo + dn * s_hi


def _lower_bound(lbp_ref):
    p = lbp_ref[...]
    m = jnp.max(p, axis=0, keepdims=True)
    e = jnp.exp(p - m)
    return e[0:1, :] / jnp.sum(e, axis=0, keepdims=True)


def _hgrn_gates(fz, lb, scratch):
    _, lf_s, k_s, _ = scratch[:4]
    f = lb + (1.0 - lb) * _sigmoid(fz)
    k_s[...] = 1.0 - f
    lf_s[...] = jnp.log(f)


HG_GROUPS = HG_CHUNK // SUBLANES_F32
N_CHUNKS = SEQ_TILE // HG_CHUNK


def _hgrn_scratch():
    wide_bf = pltpu.VMEM((SEQ_TILE, HG_W), BF16)
    return [
        pltpu.VMEM((SEQ_TILE, HG_W), F32),
        pltpu.VMEM((SEQ_TILE, HG_W), F32),
        pltpu.VMEM((SEQ_TILE, HG_W), F32),
        wide_bf,
        wide_bf, wide_bf, wide_bf, wide_bf,
        pltpu.VMEM((N_CHUNKS, SUBLANES_F32, HG_W), F32),
        pltpu.VMEM((N_CHUNKS * HG_HEADS, HG_CHUNK, HG_CHUNK), BF16),
        pltpu.VMEM((N_CHUNKS * HG_HEADS, HG_DIM, HG_DIM), F32),
        pltpu.VMEM((N_CHUNKS * HG_HEADS, HG_DIM, HG_DIM), BF16),
        pltpu.VMEM((HG_HEADS, HG_DIM, HG_DIM), F32),
    ]


def _hgrn_scan(scratch, o_ref, reverse):
    q_s, cs_s, k_s, v_s, qin_s, kin_s, qst_s, kst_s, dec_s, a_s, u_s, sall_s, st_s = scratch
    end_row = 0 if reverse else HG_CHUNK - 1
    order = range(N_CHUNKS - 1, -1, -1) if reverse else range(N_CHUNKS)

    srow = lax.broadcasted_iota(jnp.int32, (SUBLANES_F32, HG_W), 0)
    for c in range(N_CHUNKS):
        carry = None
        for j in (range(HG_GROUPS - 1, -1, -1) if reverse else range(HG_GROUPS)):
            r0 = c * HG_CHUNK + j * SUBLANES_F32
            x = cs_s[r0:r0 + SUBLANES_F32, :]
            for s in (1, 2, 4):
                if reverse:
                    x = x + jnp.where(srow < SUBLANES_F32 - s,
                                      pltpu.roll(x, SUBLANES_F32 - s, axis=0), 0.0)
                else:
                    x = x + jnp.where(srow >= s, pltpu.roll(x, s, axis=0), 0.0)
            if carry is not None:
                x = x + carry
            cs_s[r0:r0 + SUBLANES_F32, :] = x
            edge = x[0:1, :] if reverse else x[SUBLANES_F32 - 1:SUBLANES_F32, :]
            carry = jnp.broadcast_to(edge, (SUBLANES_F32, HG_W))

    for c in range(N_CHUNKS):
        rows = slice(c * HG_CHUNK, (c + 1) * HG_CHUNK)
        cs = cs_s[rows, :]
        c_end = cs[end_row:end_row + 1, :]
        c_mid = cs[HG_MID:HG_MID + 1, :]
        q = q_s[rows, :]
        k = k_s[rows, :]
        qin_s[rows, :] = (q * jnp.exp(cs - c_mid)).astype(BF16)
        kin_s[rows, :] = (k * jnp.exp(c_mid - cs)).astype(BF16)
        qst_s[rows, :] = (q * jnp.exp(cs)).astype(BF16)
        kst_s[rows, :] = (k * jnp.exp(c_end - cs)).astype(BF16)
        dec_s[c] = jnp.broadcast_to(jnp.exp(c_end), (SUBLANES_F32, HG_W))

    row = lax.broadcasted_iota(jnp.int32, (HG_CHUNK, HG_CHUNK), 0)
    col = lax.broadcasted_iota(jnp.int32, (HG_CHUNK, HG_CHUNK), 1)
    keep = (col >= row) if reverse else (col <= row)
    for c in range(N_CHUNKS):
        rows = slice(c * HG_CHUNK, (c + 1) * HG_CHUNK)
        for h in range(HG_HEADS):
            sl = slice(h * HG_DIM, (h + 1) * HG_DIM)
            a = _dot_nt(qin_s[rows, sl], kin_s[rows, sl])
            a_s[c * HG_HEADS + h] = jnp.where(keep, a, 0.0).astype(BF16)
            u_s[c * HG_HEADS + h] = _dot_tn(v_s[rows, sl], kst_s[rows, sl])

    for h in range(HG_HEADS):
        sl = slice(h * HG_DIM, (h + 1) * HG_DIM)
        st = st_s[h]
        for c in order:
            sall_s[c * HG_HEADS + h] = st.astype(BF16)
            st = st * dec_s[c, 0:1, sl] + u_s[c * HG_HEADS + h]
        st_s[h] = st

    for c in range(N_CHUNKS):
        rows = slice(c * HG_CHUNK, (c + 1) * HG_CHUNK)
        for h in range(HG_HEADS):
            sl = slice(h * HG_DIM, (h + 1) * HG_DIM)
            i = c * HG_HEADS + h
            o_ref[rows, sl] = _dot(a_s[i], v_s[rows, sl]) + _dot_nt(qst_s[rows, sl], sall_s[i])


def _mem_kv_kernel(mem_ref, g_ref, w_ref, k_ref, v_ref):
    n = _rms(mem_ref[0], g_ref[...]).astype(BF16)
    kv = _dot(n, w_ref[...])
    k_ref[0] = kv[:, :D_MODEL].astype(BF16)
    v_ref[0] = kv[:, D_MODEL:].astype(BF16)


def _mem_kv(mem, g, w_kv):
    b = mem.shape[0]
    blk = pl.BlockSpec((1, MEM_LEN, D_MODEL), lambda i: (i, 0, 0))
    return pl.pallas_call(
        _mem_kv_kernel,
        grid=(b,),
        in_specs=[blk, _const_spec((1, D_MODEL)), _const_spec((D_MODEL, 2 * D_MODEL))],
        out_specs=[blk, blk],
        out_shape=[jax.ShapeDtypeStruct((b, MEM_LEN, D_MODEL), BF16)] * 2,
        compiler_params=pltpu.CompilerParams(
            dimension_semantics=("arbitrary",), vmem_limit_bytes=VMEM_LIMIT_BYTES),
        name="mem_kv",
    )(mem, g, w_kv)


BWD_COLS = 3 * HG_W + 2 * ATT_KVW


def _mix_bwd_kernel(x_ref, cos_ref, sin_ref, g_ref, w_ref, lbp_ref,
                    ob_ref, kk_ref, vv_ref, qh_ref, vh_ref,
                    *hg):
    q_s, _, _, v_s = hg[:4]
    st_s = hg[-1]

    @pl.when(pl.program_id(1) == 0)
    def _():
        st_s[...] = jnp.zeros_like(st_s)

    n = _rms(x_ref[0], g_ref[...]).astype(BF16)
    proj = _dot(n, w_ref[...])
    q_r = proj[:, 0:HG_W]
    q = (q_r * _sigmoid(q_r)) * (HG_DIM ** -0.5)
    v = proj[:, 2 * HG_W:3 * HG_W].astype(BF16)
    q_s[...] = q
    v_s[...] = v
    qh_ref[0] = q.astype(BF16)
    vh_ref[0] = v
    _hgrn_gates(proj[:, HG_W:2 * HG_W], _lower_bound(lbp_ref), hg)

    lane = lax.broadcasted_iota(jnp.int32, (1, LANES), 1)
    first = lane < ATT_HEAD_DIM
    k0 = 3 * HG_W
    for slot in range(ROPE_SLOTS):
        rows = slice(slot * ROPE_ROWS, (slot + 1) * ROPE_ROWS)
        k_a = _rope_apply(proj[rows, k0:k0 + ATT_KVW], *_rope_block(cos_ref, sin_ref, slot))
        v_a = proj[rows, k0 + ATT_KVW:k0 + 2 * ATT_KVW]
        for src, dst in ((k_a, kk_ref), (v_a, vv_ref)):
            sw = pltpu.roll(src, ATT_HEAD_DIM, axis=1)
            dst[0, rows, 0 * LANES:1 * LANES] = jnp.where(first, src, 0.0).astype(BF16)
            dst[0, rows, 1 * LANES:2 * LANES] = jnp.where(first, 0.0, sw).astype(BF16)
            dst[0, rows, 2 * LANES:3 * LANES] = jnp.where(first, sw, 0.0).astype(BF16)
            dst[0, rows, 3 * LANES:4 * LANES] = jnp.where(first, 0.0, src).astype(BF16)

    _hgrn_scan(hg, ob_ref.at[0], reverse=True)


def _mix_bwd(x, cos, sin, g, w, lbp):
    b, s, _ = x.shape
    nt = s // SEQ_TILE
    rev = lambda bi, i: (bi, nt - 1 - i, 0)
    tile = lambda w_: pl.BlockSpec((1, SEQ_TILE, w_), rev)
    rope = pl.BlockSpec((1, ROPE_ROWS, LANES), rev)
    wide_bf = jax.ShapeDtypeStruct((b, s, HG_W), BF16)
    return pl.pallas_call(
        _mix_bwd_kernel,
        grid=(b, nt),
        in_specs=[
            tile(D_MODEL), rope, rope,
            _const_spec((1, D_MODEL)),
            _const_spec((D_MODEL, BWD_COLS)),
            _const_spec(lbp.shape),
        ],
        out_specs=[tile(HG_W), tile(4 * LANES), tile(4 * LANES), tile(HG_W), tile(HG_W)],
        out_shape=[jax.ShapeDtypeStruct((b, s, HG_W), F32), wide_bf, wide_bf, wide_bf, wide_bf],
        scratch_shapes=_hgrn_scratch(),
        compiler_params=pltpu.CompilerParams(
            dimension_semantics=("arbitrary", "arbitrary"), vmem_limit_bytes=VMEM_LIMIT_BYTES),
        name="mix_bwd",
    )(x, cos, sin, g, w, lbp)


FWD_COLS = 2 * HG_W + ATT_QW + 2 * D_MODEL
Q_BLOCKS = SEQ_TILE // ATT_BLOCK
KEY_SPAN = 3 * ATT_BLOCK


def _mix_fwd_kernel(sink_ref, x_ref, cos_ref, sin_ref, ob_ref, qh_ref, vh_ref,
                    kk_prev, kk_main, kk_next, vv_prev, vv_main, vv_next,
                    g_ref, w_ref, lbp_ref, hgn_ref, wrec_ref, watt_ref, wout_ref,
                    h_ref,
                    proj_s, of_s, qrot_s, att_s, *hg):
    ti = pl.program_id(1)
    q_s, _, _, v_s = hg[:4]
    st_s = hg[-1]

    @pl.when(ti == 0)
    def _():
        st_s[...] = jnp.zeros_like(st_s)

    x = x_ref[0]
    n = _rms(x, g_ref[...]).astype(BF16)
    proj_s[...] = _dot(n, w_ref[...])

    q_s[...] = qh_ref[0].astype(F32)
    v_s[...] = vh_ref[0]
    _hgrn_gates(proj_s[:, 0:HG_W], _lower_bound(lbp_ref), hg)
    _hgrn_scan(hg, of_s, reverse=False)
    o = of_s[...] + ob_ref[0]
    g_r = proj_s[:, HG_W:2 * HG_W]
    gate = g_r * _sigmoid(g_r)
    hgn = hgn_ref[...]
    parts = []
    for h in range(HG_HEADS):
        sl = slice(h * HG_DIM, (h + 1) * HG_DIM)
        oh = o[:, sl]
        ms = jnp.mean(oh * oh, axis=-1, keepdims=True)
        parts.append((oh * lax.rsqrt(ms + NORM_EPS) * hgn[:, sl] * gate[:, sl]).astype(BF16))
    y_rec = _dot(jnp.concatenate(parts, axis=1), wrec_ref[...])

    q0 = 2 * HG_W
    for slot in range(ROPE_SLOTS):
        rws = slice(slot * ROPE_ROWS, (slot + 1) * ROPE_ROWS)
        tables = _rope_block(cos_ref, sin_ref, slot)
        for j in range(ATT_QW // LANES):
            t = proj_s[rws, q0 + j * LANES:q0 + (j + 1) * LANES]
            qrot_s[rws, j * LANES:(j + 1) * LANES] = (
                _rope_apply(t, *tables) * (ATT_HEAD_DIM ** -0.5)).astype(BF16)

    kk_all = jnp.concatenate([kk_prev[0], kk_main[0], kk_next[0]], axis=0)
    vv_all = jnp.concatenate([vv_prev[0], vv_main[0], vv_next[0]], axis=0)
    r_i = lax.broadcasted_iota(jnp.int32, (ATT_BLOCK, KEY_SPAN), 0)
    c_i = lax.broadcasted_iota(jnp.int32, (ATT_BLOCK, KEY_SPAN), 1)
    off = c_i - r_i
    lane = lax.broadcasted_iota(jnp.int32, (1, LANES), 1)
    first = lane < ATT_HEAD_DIM
    seq_len = pl.num_programs(1) * SEQ_TILE
    for j in range(Q_BLOCKS):
        kstart = ti * SEQ_TILE + (j - 1) * ATT_BLOCK
        kpos = c_i + kstart
        valid = (off >= 0) & (off <= 2 * WINDOW) & (kpos >= 0) & (kpos < seq_len)
        bias = jnp.where(valid, 0.0, -jnp.inf)
        rows = slice(j * ATT_BLOCK, (j + 1) * ATT_BLOCK)
        keys = slice(j * ATT_BLOCK, j * ATT_BLOCK + KEY_SPAN)
        for g in range(ATT_KV_HEADS):
            ca = slice(2 * g * LANES, (2 * g + 1) * LANES)
            cb = slice((2 * g + 1) * LANES, (2 * g + 2) * LANES)
            lhs = jnp.concatenate([qrot_s[rows, ca], qrot_s[rows, cb]], axis=0)
            k_bd = jnp.concatenate([kk_all[keys, ca], kk_all[keys, cb]], axis=0)
            v_bd = jnp.concatenate([vv_all[keys, ca], vv_all[keys, cb]], axis=0)
            s = _dot_nt(lhs, k_bd)
            p_rows, inv_rows = [], []
            for half in range(2):
                p_seg, inv_seg = [], []
                for seg in range(2):
                    head = 4 * g + 2 * half + seg
                    ss = s[half * ATT_BLOCK:(half + 1) * ATT_BLOCK,
                           seg * KEY_SPAN:(seg + 1) * KEY_SPAN] + bias
                    sk = sink_ref[head]
                    m = jnp.maximum(jnp.max(ss, axis=-1, keepdims=True), sk)
                    p = jnp.exp(ss - m)
                    den = jnp.sum(p, axis=-1, keepdims=True) + jnp.exp(sk - m)
                    p_seg.append(p.astype(BF16))
                    inv_seg.append(1.0 / den)
                p_rows.append(jnp.concatenate(p_seg, axis=1))
                inv_rows.append(jnp.where(first, inv_seg[0], inv_seg[1]))
            pv = _dot(jnp.concatenate(p_rows, axis=0), v_bd)
            att_s[rows, ca] = (pv[:ATT_BLOCK] * inv_rows[0]).astype(BF16)
            att_s[rows, cb] = (pv[ATT_BLOCK:] * inv_rows[1]).astype(BF16)
    y_att = _dot(att_s[...], watt_ref[...])

    g0 = 2 * HG_W + ATT_QW
    merged = (_sigmoid(proj_s[:, g0:g0 + D_MODEL]) * y_rec
              + _sigmoid(proj_s[:, g0 + D_MODEL:g0 + 2 * D_MODEL]) * y_att)
    h_ref[0] = x + _dot(merged.astype(BF16), wout_ref[...])


def _mix_fwd(sink, x, cos, sin, ob, qh, vh, kk, vv, g, w, lbp, hgn, wrec, watt, wout):
    b, s, _ = x.shape
    nt = s // SEQ_TILE
    r = SEQ_TILE // ATT_BLOCK
    nblk = s // ATT_BLOCK
    main = lambda bi, i: (bi, i, 0)
    prev = lambda bi, i: (bi, jnp.maximum(i * r - 1, 0), 0)
    nxt = lambda bi, i: (bi, jnp.minimum((i + 1) * r, nblk - 1), 0)
    halo = lambda imap: pl.BlockSpec((1, ATT_BLOCK, 4 * LANES), imap)
    tile = lambda w_: pl.BlockSpec((1, SEQ_TILE, w_), main)
    rope = pl.BlockSpec((1, ROPE_ROWS, LANES), main)
    return pl.pallas_call(
        _mix_fwd_kernel,
        grid=(b, nt),
        in_specs=[
            pl.BlockSpec(memory_space=pltpu.SMEM),
            tile(D_MODEL), rope, rope, tile(HG_W), tile(HG_W), tile(HG_W),
            halo(prev), tile(4 * LANES), halo(nxt),
            halo(prev), tile(4 * LANES), halo(nxt),
            _const_spec((1, D_MODEL)),
            _const_spec((D_MODEL, FWD_COLS)),
            _const_spec(lbp.shape),
            _const_spec((1, HG_W)),
            _const_spec((HG_W, D_MODEL)),
            _const_spec((ATT_QW, D_MODEL)),
            _const_spec((D_MODEL, D_MODEL)),
        ],
        out_specs=tile(D_MODEL),
        out_shape=jax.ShapeDtypeStruct((b, s, D_MODEL), F32),
        scratch_shapes=[
            pltpu.VMEM((SEQ_TILE, FWD_COLS), F32),
            pltpu.VMEM((SEQ_TILE, HG_W), F32),
            pltpu.VMEM((SEQ_TILE, ATT_QW), BF16),
            pltpu.VMEM((SEQ_TILE, ATT_QW), BF16),
        ] + _hgrn_scratch(),
        compiler_params=pltpu.CompilerParams(
            dimension_semantics=("arbitrary", "arbitrary"), vmem_limit_bytes=VMEM_LIMIT_BYTES),
        name="mix_fwd",
    )(sink, x, cos, sin, ob, qh, vh, kk, kk, kk, vv, vv, vv, g, w, lbp, hgn, wrec, watt, wout)


def _mem_xattn_kernel(h_ref, k_ref, v_ref, g_ref, wq_ref, wo_ref, o_ref, att_s):
    h = h_ref[0]
    n = _rms(h, g_ref[...]).astype(BF16)
    q = (_dot(n, wq_ref[...]) * (MEM_HEAD_DIM ** -0.5)).astype(BF16)
    for hd in range(MEM_HEADS):
        sl = slice(hd * MEM_HEAD_DIM, (hd + 1) * MEM_HEAD_DIM)
        s = _dot_nt(q[:, sl], k_ref[0, :, sl])
        m = jnp.max(s, axis=-1, keepdims=True)
        p = jnp.exp(s - m)
        inv = 1.0 / jnp.sum(p, axis=-1, keepdims=True)
        att_s[:, sl] = (_dot(p.astype(BF16), v_ref[0, :, sl]) * inv).astype(BF16)
    o_ref[0] = h + _dot(att_s[...], wo_ref[...])


def _mem_xattn(h, k_mem, v_mem, g, wq, wo):
    b, s, _ = h.shape
    tile = pl.BlockSpec((1, SEQ_TILE, D_MODEL), lambda bi, i: (bi, i, 0))
    kv = pl.BlockSpec((1, MEM_LEN, D_MODEL), lambda bi, i: (bi, 0, 0))
    return pl.pallas_call(
        _mem_xattn_kernel,
        grid=(b, s // SEQ_TILE),
        in_specs=[tile, kv, kv, _const_spec((1, D_MODEL)),
                  _const_spec((D_MODEL, D_MODEL)), _const_spec((D_MODEL, D_MODEL))],
        out_specs=tile,
        out_shape=jax.ShapeDtypeStruct((b, s, D_MODEL), F32),
        scratch_shapes=[pltpu.VMEM((SEQ_TILE, D_MODEL), BF16)],
        compiler_params=pltpu.CompilerParams(
            dimension_semantics=("arbitrary", "arbitrary"), vmem_limit_bytes=VMEM_LIMIT_BYTES),
        name="mem_xattn",
    )(h, k_mem, v_mem, g, wq, wo)


FFN_ROWS = SEQ_TILE + 2 * SUBLANES_F32


def _conv_ffn_kernel(hp_ref, h_ref, hn_ref, g_ref, wu_ref, wg_ref, cw_ref, cb_ref, wd_ref, fn_ref,
                     o_ref):
    ti = pl.program_id(1)
    h = h_ref[0]
    hcat = jnp.concatenate([hp_ref[0], h, hn_ref[0]], axis=0)
    nf = _rms(hcat, g_ref[...])
    n_all = nf.astype(BF16)
    n_main = nf[SUBLANES_F32:SUBLANES_F32 + SEQ_TILE].astype(BF16)
    ridx = lax.broadcasted_iota(jnp.int32, (FFN_ROWS, 1), 0)
    pad_lo = (ridx < SUBLANES_F32) & (ti == 0)
    pad_hi = (ridx >= SUBLANES_F32 + SEQ_TILE) & (ti == pl.num_programs(1) - 1)
    rowmask = jnp.where(pad_lo | pad_hi, 0.0, 1.0)
    main = slice(SUBLANES_F32, SUBLANES_F32 + SEQ_TILE)
    acc = jnp.zeros((SEQ_TILE, D_MODEL), F32)
    c0 = 0
    for width in FF_CHUNKS:
        cols = slice(c0, c0 + width)
        u = _dot(n_main, wu_ref[:, cols])
        gx = _dot(n_all, wg_ref[:, cols]) * rowmask
        g_prev = pltpu.roll(gx, 1, axis=0)[main]
        g_next = pltpu.roll(gx, FFN_ROWS - 1, axis=0)[main]
        gc = (g_prev * cw_ref[0:1, cols] + gx[main] * cw_ref[1:2, cols]
              + g_next * cw_ref[2:3, cols] + cb_ref[:, cols])
        a = (gc * _sigmoid(gc) * u).astype(BF16)
        acc = acc + _dot(a, wd_ref[cols, :])
        c0 += width
    o_ref[0] = _rms(h + acc, fn_ref[...])


def _conv_ffn(h, g, wu, wg, cw, cb, wd, fn):
    b, s, _ = h.shape
    r = SEQ_TILE // SUBLANES_F32
    nblk = s // SUBLANES_F32
    tile = pl.BlockSpec((1, SEQ_TILE, D_MODEL), lambda bi, i: (bi, i, 0))
    prev = pl.BlockSpec((1, SUBLANES_F32, D_MODEL), lambda bi, i: (bi, jnp.maximum(i * r - 1, 0), 0))
    nxt = pl.BlockSpec((1, SUBLANES_F32, D_MODEL),
                       lambda bi, i: (bi, jnp.minimum((i + 1) * r, nblk - 1), 0))
    return pl.pallas_call(
        _conv_ffn_kernel,
        grid=(b, s // SEQ_TILE),
        in_specs=[prev, tile, nxt, _const_spec((1, D_MODEL)),
                  _const_spec((D_MODEL, D_FF)), _const_spec((D_MODEL, D_FF)),
                  _const_spec((3, D_FF)), _const_spec((1, D_FF)),
                  _const_spec((D_FF, D_MODEL)), _const_spec((1, D_MODEL))],
        out_specs=tile,
        out_shape=jax.ShapeDtypeStruct((b, s, D_MODEL), F32),
        compiler_params=pltpu.CompilerParams(
            dimension_semantics=("arbitrary", "arbitrary"), vmem_limit_bytes=VMEM_LIMIT_BYTES),
        name="conv_ffn",
    )(h, h, h, g, wu, wg, cw, cb, wd, fn)


def kernel(x, mem, positions, norm_mix, w_in, lower_bounds, hg_norm, attn_sink, w_br_rec, w_br_att,
           w_mix_out, norm_mem, norm_mem_kv, w_mem_q, w_mem_kv, w_mem_o, norm_ffn, w_up, conv_w,
           conv_b, w_down, final_norm):
    assert x.shape[1] % SEQ_TILE == 0 and x.shape[2] == D_MODEL and w_in.shape[0] == 1
    bf = lambda a: a.astype(BF16)
    row = lambda a: a.reshape(1, -1).astype(F32)

    o = 0
    cols = {}
    for name, width in (("q_r", HG_W), ("fz_f", HG_W), ("fz_b", HG_W), ("i_r", HG_W), ("g_r", HG_W),
                        ("q_a", ATT_QW), ("k_a", ATT_KVW), ("v_a", ATT_KVW),
                        ("gate_r", D_MODEL), ("gate_a", D_MODEL)):
        cols[name] = w_in[0][:, o:o + width]
        o += width
    w_bwd = bf(jnp.concatenate([cols[k] for k in ("q_r", "fz_b", "i_r", "k_a", "v_a")], axis=1))
    w_fwd = bf(jnp.concatenate(
        [cols[k] for k in ("fz_f", "g_r", "q_a", "gate_r", "gate_a")], axis=1))

    cos, sin = _rope_tables(positions)
    g_mix = row(norm_mix[0])

    k_mem, v_mem = _mem_kv(mem, row(norm_mem_kv[0]), bf(w_mem_kv[0]))
    o_bwd, kk, vv, qh, vh = _mix_bwd(x, cos, sin, g_mix, w_bwd, lower_bounds[1].astype(F32))
    h1 = _mix_fwd(attn_sink[0].astype(F32), x, cos, sin, o_bwd, qh, vh, kk, vv, g_mix, w_fwd,
                  lower_bounds[0].astype(F32), row(hg_norm[0]), bf(w_br_rec[0]), bf(w_br_att[0]),
                  bf(w_mix_out[0]))
    h2 = _mem_xattn(h1, k_mem, v_mem, row(norm_mem[0]), bf(w_mem_q[0]), bf(w_mem_o[0]))
    return _conv_ffn(h2, row(norm_ffn[0]), bf(w_up[0][:, :D_FF]), bf(w_up[0][:, D_FF:]),
                     conv_w[0].astype(F32), row(conv_b[0]), bf(w_down[0]), row(final_norm))
```

```python
import functools

import jax
import jax.numpy as jnp
from jax import lax
from jax.experimental import pallas as pl
from jax.experimental.pallas import tpu as pltpu

F32 = jnp.float32
BF16 = jnp.bfloat16

D_MODEL = 1024
MEM_LEN = 256
HG_HEADS = 4
HG_DIM = 128
HG_W = HG_HEADS * HG_DIM
HG_CHUNK = 64
HG_MID = HG_CHUNK // 2
ATT_Q_HEADS = 8
ATT_KV_HEADS = 2
ATT_HEAD_DIM = 64
ATT_QW = ATT_Q_HEADS * ATT_HEAD_DIM
ATT_KVW = ATT_KV_HEADS * ATT_HEAD_DIM
WINDOW = 128
ATT_BLOCK = 128
ROPE_THETA = 500000.0
ROT_DIM = ATT_HEAD_DIM // 4
ROT_HALF = ROT_DIM // 2
MEM_HEADS = 4
MEM_HEAD_DIM = D_MODEL // MEM_HEADS
D_FF = 2816
NORM_EPS = 1e-6
LOG2E = 1.4426950408889634

LANES = 128
SUBLANES_F32 = 8
VMEM_LIMIT_BYTES = 56 * 1024 * 1024

SEQ_TILE = 512
FF_CHUNKS = (768, 768, 768, 512)


def _const_spec(shape):
    nd = len(shape)
    return pl.BlockSpec(shape, lambda *_: (0,) * nd, pipeline_mode=pl.Buffered(1))


def _rms(x, g):
    ms = jnp.mean(x * x, axis=-1, keepdims=True)
    return x * lax.rsqrt(ms + NORM_EPS) * g


def _sigmoid(x):
    return 0.5 * jnp.tanh(0.5 * x) + 0.5


def _dot(a, b):
    return jnp.dot(a, b, preferred_element_type=F32)


def _dot_nt(a, b):
    return lax.dot_general(a, b, (((1,), (1,)), ((), ())), preferred_element_type=F32)


def _dot_tn(a, b):
    return lax.dot_general(a, b, (((0,), (0,)), ((), ())), preferred_element_type=F32)


ROPE_SLOTS = LANES // ROT_DIM
ROPE_ROWS = SEQ_TILE // ROPE_SLOTS


def _rope_kernel(pos_ref, cos_ref, sin_ref):
    lane = lax.broadcasted_iota(jnp.int32, (1, LANES), 1)
    fidx = lane % ROT_HALF
    inv_freq = jnp.zeros((1, LANES), F32)
    for i in range(ROT_HALF):
        inv_freq = jnp.where(fidx == i, 1.0 / (ROPE_THETA ** (2.0 * i / ROT_DIM)), inv_freq)
    ang = pos_ref[...].astype(F32) * inv_freq
    cos_ref[...] = jnp.cos(ang)
    sin_ref[...] = jnp.sin(ang)


def _rope_tables(positions):
    b, s = positions.shape
    nt = s // SEQ_TILE
    p = positions.reshape(b, nt, ROPE_SLOTS, ROPE_ROWS).transpose(0, 1, 3, 2)
    p = jnp.broadcast_to(p[..., None], (b, nt, ROPE_ROWS, ROPE_SLOTS, ROT_DIM))
    p = p.reshape(b * nt * ROPE_ROWS, LANES)
    full = pl.BlockSpec(p.shape, lambda: (0, 0))
    cos, sin = pl.pallas_call(
        _rope_kernel,
        in_specs=[full],
        out_specs=[full, full],
        out_shape=[jax.ShapeDtypeStruct(p.shape, F32)] * 2,
        name="rope_tab",
    )(p)
    return cos.reshape(b, nt * ROPE_ROWS, LANES), sin.reshape(b, nt * ROPE_ROWS, LANES)


def _rope_block(cos_ref, sin_ref, slot):
    lane = lax.broadcasted_iota(jnp.int32, (1, LANES), 1)
    d = lane % ATT_HEAD_DIM
    first = lane < ATT_HEAD_DIM

    def spread(packed):
        sh = (LANES - slot * ROT_DIM) % LANES
        a = pltpu.roll(packed, sh, axis=1) if sh else packed
        b = pltpu.roll(packed, (sh + ATT_HEAD_DIM) % LANES, axis=1)
        return jnp.where(first, a, b)

    cos = spread(cos_ref[0])
    sin = spread(sin_ref[0])
    c = jnp.where(d < ROT_DIM, cos, 1.0)
    s_lo = jnp.where(d < ROT_HALF, -sin, 0.0)
    s_hi = jnp.where((d >= ROT_HALF) & (d < ROT_DIM), sin, 0.0)
    return c, s_lo, s_hi


def _rope_apply(t, c, s_lo, s_hi):
    up = pltpu.roll(t, LANES - ROT_HALF, axis=1)
    dn = pltpu.roll(t, ROT_HALF, axis=1)
    return t * c + up * s_lo + dn * s_hi


def _lower_bound(lbp_ref):
    p = lbp_ref[...]
    m = jnp.max(p, axis=0, keepdims=True)
    e = jnp.exp(p - m)
    return e[0:1, :] / jnp.sum(e, axis=0, keepdims=True)


def _hgrn_gates(fz, lb, scratch):
    _, lf_s, k_s, _ = scratch[:4]
    f = lb + (1.0 - lb) * _sigmoid(fz)
    k_s[...] = 1.0 - f
    lf_s[...] = jnp.log2(f)


HG_GROUPS = HG_CHUNK // SUBLANES_F32
N_CHUNKS = SEQ_TILE // HG_CHUNK


def _hgrn_scratch():
    wide_bf = pltpu.VMEM((SEQ_TILE, HG_W), BF16)
    return [
        pltpu.VMEM((SEQ_TILE, HG_W), F32),
        pltpu.VMEM((SEQ_TILE, HG_W), F32),
        pltpu.VMEM((SEQ_TILE, HG_W), F32),
        wide_bf,
        wide_bf, wide_bf, wide_bf, wide_bf,
        pltpu.VMEM((N_CHUNKS, SUBLANES_F32, HG_W), F32),
        pltpu.VMEM((N_CHUNKS * HG_HEADS, HG_CHUNK, HG_CHUNK), BF16),
        pltpu.VMEM((N_CHUNKS * HG_HEADS, HG_DIM, HG_DIM), F32),
        pltpu.VMEM((N_CHUNKS * HG_HEADS, HG_DIM, HG_DIM), BF16),
        pltpu.VMEM((HG_HEADS, HG_DIM, HG_DIM), F32),
    ]


def _hgrn_stages(scratch, o_ref, reverse):
    return (functools.partial(_hgrn_operands, scratch, reverse),
            functools.partial(_hgrn_local, scratch, reverse),
            functools.partial(_hgrn_recurrence, scratch, reverse),
            functools.partial(_hgrn_outputs, scratch, o_ref))


def _hgrn_operands(scratch, reverse):
    q_s, cs_s, k_s, _, qin_s, kin_s, qst_s, kst_s, dec_s = scratch[:9]
    end_row = 0 if reverse else HG_CHUNK - 1

    srow = lax.broadcasted_iota(jnp.int32, (SUBLANES_F32, HG_W), 0)
    for c in range(N_CHUNKS):
        carry = None
        for j in (range(HG_GROUPS - 1, -1, -1) if reverse else range(HG_GROUPS)):
            r0 = c * HG_CHUNK + j * SUBLANES_F32
            x = cs_s[r0:r0 + SUBLANES_F32, :]
            for s in (1, 2, 4):
                if reverse:
                    x = x + jnp.where(srow < SUBLANES_F32 - s,
                                      pltpu.roll(x, SUBLANES_F32 - s, axis=0), 0.0)
                else:
                    x = x + jnp.where(srow >= s, pltpu.roll(x, s, axis=0), 0.0)
            if carry is not None:
                x = x + carry
            cs_s[r0:r0 + SUBLANES_F32, :] = x
            edge = x[0:1, :] if reverse else x[SUBLANES_F32 - 1:SUBLANES_F32, :]
            carry = jnp.broadcast_to(edge, (SUBLANES_F32, HG_W))

    for c in range(N_CHUNKS):
        rows = slice(c * HG_CHUNK, (c + 1) * HG_CHUNK)
        cs = cs_s[rows, :]
        c_end = cs[end_row:end_row + 1, :]
        c_mid = cs[HG_MID:HG_MID + 1, :]
        q = q_s[rows, :]
        k = k_s[rows, :]
        qin_s[rows, :] = (q * jnp.exp2(cs - c_mid)).astype(BF16)
        kin_s[rows, :] = (k * jnp.exp2(c_mid - cs)).astype(BF16)
        qst_s[rows, :] = (q * jnp.exp2(cs)).astype(BF16)
        kst_s[rows, :] = (k * jnp.exp2(c_end - cs)).astype(BF16)
        dec_s[c] = jnp.broadcast_to(jnp.exp2(c_end), (SUBLANES_F32, HG_W))


def _hgrn_local(scratch, reverse):
    _, _, _, v_s, qin_s, kin_s, _, kst_s, _, a_s, u_s = scratch[:11]
    row = lax.broadcasted_iota(jnp.int32, (HG_CHUNK, HG_CHUNK), 0)
    col = lax.broadcasted_iota(jnp.int32, (HG_CHUNK, HG_CHUNK), 1)
    keep = (col >= row) if reverse else (col <= row)
    for c in range(N_CHUNKS):
        rows = slice(c * HG_CHUNK, (c + 1) * HG_CHUNK)
        for h in range(HG_HEADS):
            sl = slice(h * HG_DIM, (h + 1) * HG_DIM)
            a = _dot_nt(qin_s[rows, sl], kin_s[rows, sl])
            a_s[c * HG_HEADS + h] = jnp.where(keep, a, 0.0).astype(BF16)
            u_s[c * HG_HEADS + h] = _dot_tn(v_s[rows, sl], kst_s[rows, sl])


def _hgrn_recurrence(scratch, reverse):
    dec_s, _, u_s, sall_s, st_s = scratch[8:]
    order = range(N_CHUNKS - 1, -1, -1) if reverse else range(N_CHUNKS)
    for h in range(HG_HEADS):
        sl = slice(h * HG_DIM, (h + 1) * HG_DIM)
        st = st_s[h]
        for c in order:
            sall_s[c * HG_HEADS + h] = st.astype(BF16)
            st = st * dec_s[c, 0:1, sl] + u_s[c * HG_HEADS + h]
        st_s[h] = st


def _hgrn_outputs(scratch, o_ref):
    v_s, qst_s, a_s, sall_s = scratch[3], scratch[6], scratch[9], scratch[11]
    for c in range(N_CHUNKS):
        rows = slice(c * HG_CHUNK, (c + 1) * HG_CHUNK)
        for h in range(HG_HEADS):
            sl = slice(h * HG_DIM, (h + 1) * HG_DIM)
            i = c * HG_HEADS + h
            o_ref[rows, sl] = _dot(a_s[i], v_s[rows, sl]) + _dot_nt(qst_s[rows, sl], sall_s[i])


def _mem_kv_kernel(mem_ref, g_ref, w_ref, k_ref, v_ref):
    n = _rms(mem_ref[0], g_ref[...]).astype(BF16)
    kv = _dot(n, w_ref[...])
    k_ref[0] = kv[:, :D_MODEL].astype(BF16)
    v_ref[0] = kv[:, D_MODEL:].astype(BF16)


def _mem_kv(mem, g, w_kv):
    b = mem.shape[0]
    blk = pl.BlockSpec((1, MEM_LEN, D_MODEL), lambda i: (i, 0, 0))
    return pl.pallas_call(
        _mem_kv_kernel,
        grid=(b,),
        in_specs=[blk, _const_spec((1, D_MODEL)), _const_spec((D_MODEL, 2 * D_MODEL))],
        out_specs=[blk, blk],
        out_shape=[jax.ShapeDtypeStruct((b, MEM_LEN, D_MODEL), BF16)] * 2,
        compiler_params=pltpu.CompilerParams(
            dimension_semantics=("arbitrary",), vmem_limit_bytes=VMEM_LIMIT_BYTES),
        name="mem_kv",
    )(mem, g, w_kv)


BWD_COLS = 3 * HG_W + 2 * ATT_KVW


def _mix_bwd_kernel(x_ref, cos_ref, sin_ref, g_ref, w_ref, lbp_ref,
                    ob_ref, kk_ref, vv_ref, qh_ref, vh_ref,
                    *hg):
    q_s, _, _, v_s = hg[:4]
    st_s = hg[-1]

    @pl.when(pl.program_id(1) == 0)
    def _():
        st_s[...] = jnp.zeros_like(st_s)

    n = _rms(x_ref[0], g_ref[...]).astype(BF16)
    project = lambda lo, hi: _dot(n, w_ref[:, lo:hi])
    operands, local, recurrence, outputs = _hgrn_stages(hg, ob_ref.at[0], reverse=True)

    fz = project(HG_W, 2 * HG_W)
    q_r = project(0, HG_W)
    _hgrn_gates(fz, _lower_bound(lbp_ref), hg)
    v = project(2 * HG_W, 3 * HG_W).astype(BF16)
    q = (q_r * _sigmoid(q_r)) * (HG_DIM ** -0.5)
    q_s[...] = q
    v_s[...] = v
    qh_ref[0] = q.astype(BF16)
    vh_ref[0] = v
    kv = project(3 * HG_W, BWD_COLS)
    operands()

    lane = lax.broadcasted_iota(jnp.int32, (1, LANES), 1)
    first = lane < ATT_HEAD_DIM

    def emit_kv(slots):
        for slot in slots:
            rows = slice(slot * ROPE_ROWS, (slot + 1) * ROPE_ROWS)
            k_a = _rope_apply(kv[rows, 0:ATT_KVW], *_rope_block(cos_ref, sin_ref, slot))
            v_a = kv[rows, ATT_KVW:2 * ATT_KVW]
            for src, dst in ((k_a, kk_ref), (v_a, vv_ref)):
                sw = pltpu.roll(src, ATT_HEAD_DIM, axis=1)
                dst[0, rows, 0 * LANES:1 * LANES] = jnp.where(first, src, 0.0).astype(BF16)
                dst[0, rows, 1 * LANES:2 * LANES] = jnp.where(first, 0.0, sw).astype(BF16)
                dst[0, rows, 2 * LANES:3 * LANES] = jnp.where(first, sw, 0.0).astype(BF16)
                dst[0, rows, 3 * LANES:4 * LANES] = jnp.where(first, 0.0, src).astype(BF16)

    local()
    emit_kv(range(0, ROPE_SLOTS // 2))
    recurrence()
    emit_kv(range(ROPE_SLOTS // 2, ROPE_SLOTS))
    outputs()


def _mix_bwd(x, cos, sin, g, w, lbp):
    b, s, _ = x.shape
    nt = s // SEQ_TILE
    rev = lambda bi, i: (bi, nt - 1 - i, 0)
    tile = lambda w_: pl.BlockSpec((1, SEQ_TILE, w_), rev)
    rope = pl.BlockSpec((1, ROPE_ROWS, LANES), rev)
    wide_bf = jax.ShapeDtypeStruct((b, s, HG_W), BF16)
    return pl.pallas_call(
        _mix_bwd_kernel,
        grid=(b, nt),
        in_specs=[
            tile(D_MODEL), rope, rope,
            _const_spec((1, D_MODEL)),
            _const_spec((D_MODEL, BWD_COLS)),
            _const_spec(lbp.shape),
        ],
        out_specs=[tile(HG_W), tile(4 * LANES), tile(4 * LANES), tile(HG_W), tile(HG_W)],
        out_shape=[jax.ShapeDtypeStruct((b, s, HG_W), F32), wide_bf, wide_bf, wide_bf, wide_bf],
        scratch_shapes=_hgrn_scratch(),
        compiler_params=pltpu.CompilerParams(
            dimension_semantics=("arbitrary", "arbitrary"), vmem_limit_bytes=VMEM_LIMIT_BYTES),
        name="mix_bwd",
    )(x, cos, sin, g, w, lbp)


FWD_COLS = 2 * HG_W + ATT_QW + 2 * D_MODEL
Q_BLOCKS = SEQ_TILE // ATT_BLOCK
KEY_SPAN = 3 * ATT_BLOCK


def _mix_fwd_kernel(sink_ref, x_ref, cos_ref, sin_ref, ob_ref, qh_ref, vh_ref,
                    kk_prev, kk_main, kk_next, vv_prev, vv_main, vv_next,
                    g_ref, w_ref, lbp_ref, hgn_ref, wrec_ref, watt_ref, wout_ref,
                    h_ref,
                    proj_s, of_s, qrot_s, att_s, *hg):
    ti = pl.program_id(1)
    q_s, _, _, v_s = hg[:4]
    st_s = hg[-1]

    @pl.when(ti == 0)
    def _():
        st_s[...] = jnp.zeros_like(st_s)

    x = x_ref[0]
    n = _rms(x, g_ref[...]).astype(BF16)

    def project(lo, hi):
        proj_s[:, lo:hi] = _dot(n, w_ref[:, lo:hi])

    g0 = 2 * HG_W + ATT_QW
    operands, local, recurrence, outputs = _hgrn_stages(hg, of_s, reverse=False)

    def recurrent_branch():
        o = of_s[...] + ob_ref[0]
        g_r = proj_s[:, HG_W:2 * HG_W]
        gate = g_r * _sigmoid(g_r)
        hgn = hgn_ref[...]
        parts = []
        for h in range(HG_HEADS):
            sl = slice(h * HG_DIM, (h + 1) * HG_DIM)
            oh = o[:, sl]
            ms = jnp.mean(oh * oh, axis=-1, keepdims=True)
            parts.append((oh * lax.rsqrt(ms + NORM_EPS) * hgn[:, sl] * gate[:, sl]).astype(BF16))
        return jnp.concatenate(parts, axis=1)

    def rotate_queries():
        q0 = 2 * HG_W
        for slot in range(ROPE_SLOTS):
            rws = slice(slot * ROPE_ROWS, (slot + 1) * ROPE_ROWS)
            tables = _rope_block(cos_ref, sin_ref, slot)
            for j in range(ATT_QW // LANES):
                t = proj_s[rws, q0 + j * LANES:q0 + (j + 1) * LANES]
                qrot_s[rws, j * LANES:(j + 1) * LANES] = (
                    _rope_apply(t, *tables) * (ATT_HEAD_DIM ** -0.5 * LOG2E)).astype(BF16)

    project(0, HG_W)
    project(2 * HG_W, g0)
    q_s[...] = qh_ref[0].astype(F32)
    v_s[...] = vh_ref[0]
    _hgrn_gates(proj_s[:, 0:HG_W], _lower_bound(lbp_ref), hg)
    project(HG_W, 2 * HG_W)
    operands()
    project(g0, g0 + D_MODEL)
    rotate_queries()

    kk_all = jnp.concatenate([kk_prev[0], kk_main[0], kk_next[0]], axis=0)
    vv_all = jnp.concatenate([vv_prev[0], vv_main[0], vv_next[0]], axis=0)
    r_i = lax.broadcasted_iota(jnp.int32, (ATT_BLOCK, KEY_SPAN), 0)
    c_i = lax.broadcasted_iota(jnp.int32, (ATT_BLOCK, KEY_SPAN), 1)
    off = c_i - r_i
    lane = lax.broadcasted_iota(jnp.int32, (1, LANES), 1)
    first = lane < ATT_HEAD_DIM
    seq_len = pl.num_programs(1) * SEQ_TILE
    seg_of_row = lax.broadcasted_iota(jnp.int32, (2 * KEY_SPAN, LANES), 0) // KEY_SPAN
    seg_of_lane = lax.broadcasted_iota(jnp.int32, (2 * KEY_SPAN, LANES), 1) // ATT_HEAD_DIM
    ones_bd = jnp.where(seg_of_row == seg_of_lane, 1.0, 0.0).astype(BF16)
    units = [(j, g) for j in range(Q_BLOCKS) for g in range(ATT_KV_HEADS)]

    def scores(j, g):
        rows = slice(j * ATT_BLOCK, (j + 1) * ATT_BLOCK)
        keys = slice(j * ATT_BLOCK, j * ATT_BLOCK + KEY_SPAN)
        ca = slice(2 * g * LANES, (2 * g + 1) * LANES)
        cb = slice((2 * g + 1) * LANES, (2 * g + 2) * LANES)
        lhs = jnp.concatenate([qrot_s[rows, ca], qrot_s[rows, cb]], axis=0)
        k_bd = jnp.concatenate([kk_all[keys, ca], kk_all[keys, cb]], axis=0)
        return _dot_nt(lhs, k_bd)

    pending = [scores(*units[0])]

    def attend(u):
        j, g = units[u]
        s = pending.pop()
        if u + 1 < len(units):
            pending.append(scores(*units[u + 1]))
        kstart = ti * SEQ_TILE + (j - 1) * ATT_BLOCK
        kpos = c_i + kstart
        valid = (off >= 0) & (off <= 2 * WINDOW) & (kpos >= 0) & (kpos < seq_len)
        bias = jnp.where(valid, 0.0, -jnp.inf)
        rows = slice(j * ATT_BLOCK, (j + 1) * ATT_BLOCK)
        keys = slice(j * ATT_BLOCK, j * ATT_BLOCK + KEY_SPAN)
        ca = slice(2 * g * LANES, (2 * g + 1) * LANES)
        cb = slice((2 * g + 1) * LANES, (2 * g + 2) * LANES)
        v_bd = jnp.concatenate([vv_all[keys, ca], vv_all[keys, cb]], axis=0)
        v_ext = jnp.concatenate([v_bd, ones_bd], axis=1)
        p_rows, sink_rows = [], []
        for half in range(2):
            p_seg, sink_seg = [], []
            for seg in range(2):
                head = 4 * g + 2 * half + seg
                ss = s[half * ATT_BLOCK:(half + 1) * ATT_BLOCK,
                       seg * KEY_SPAN:(seg + 1) * KEY_SPAN] + bias
                sk = sink_ref[head] * LOG2E
                m = jnp.maximum(jnp.max(ss, axis=-1, keepdims=True), sk)
                p_seg.append(jnp.exp2(ss - m).astype(BF16))
                sink_seg.append(jnp.exp2(sk - m))
            p_rows.append(jnp.concatenate(p_seg, axis=1))
            sink_rows.append(jnp.where(first, sink_seg[0], sink_seg[1]))
        pv = _dot(jnp.concatenate(p_rows, axis=0), v_ext)
        den = pv[:, LANES:] + jnp.concatenate(sink_rows, axis=0)
        out = pv[:, :LANES] / den
        att_s[rows, ca] = out[:ATT_BLOCK].astype(BF16)
        att_s[rows, cb] = out[ATT_BLOCK:].astype(BF16)

    quarter = len(units) // 4
    local()
    for u in range(0, quarter):
        attend(u)
    recurrence()
    for u in range(quarter, 2 * quarter):
        attend(u)
    outputs()
    for u in range(2 * quarter, 3 * quarter):
        attend(u)
    rec_in = recurrent_branch()
    project(g0 + D_MODEL, g0 + 2 * D_MODEL)
    for u in range(3 * quarter, len(units)):
        attend(u)
    y_rec = _dot(rec_in, wrec_ref[...])
    y_att = _dot(att_s[...], watt_ref[...])

    merged = (_sigmoid(proj_s[:, g0:g0 + D_MODEL]) * y_rec
              + _sigmoid(proj_s[:, g0 + D_MODEL:g0 + 2 * D_MODEL]) * y_att)
    h_ref[0] = x + _dot(merged.astype(BF16), wout_ref[...])


def _mix_fwd(sink, x, cos, sin, ob, qh, vh, kk, vv, g, w, lbp, hgn, wrec, watt, wout):
    b, s, _ = x.shape
    nt = s // SEQ_TILE
    r = SEQ_TILE // ATT_BLOCK
    nblk = s // ATT_BLOCK
    main = lambda bi, i: (bi, i, 0)
    prev = lambda bi, i: (bi, jnp.maximum(i * r - 1, 0), 0)
    nxt = lambda bi, i: (bi, jnp.minimum((i + 1) * r, nblk - 1), 0)
    halo = lambda imap: pl.BlockSpec((1, ATT_BLOCK, 4 * LANES), imap)
    tile = lambda w_: pl.BlockSpec((1, SEQ_TILE, w_), main)
    rope = pl.BlockSpec((1, ROPE_ROWS, LANES), main)
    return pl.pallas_call(
        _mix_fwd_kernel,
        grid=(b, nt),
        in_specs=[
            pl.BlockSpec(memory_space=pltpu.SMEM),
            tile(D_MODEL), rope, rope, tile(HG_W), tile(HG_W), tile(HG_W),
            halo(prev), tile(4 * LANES), halo(nxt),
            halo(prev), tile(4 * LANES), halo(nxt),
            _const_spec((1, D_MODEL)),
            _const_spec((D_MODEL, FWD_COLS)),
            _const_spec(lbp.shape),
            _const_spec((1, HG_W)),
            _const_spec((HG_W, D_MODEL)),
            _const_spec((ATT_QW, D_MODEL)),
            _const_spec((D_MODEL, D_MODEL)),
        ],
        out_specs=tile(D_MODEL),
        out_shape=jax.ShapeDtypeStruct((b, s, D_MODEL), F32),
        scratch_shapes=[
            pltpu.VMEM((SEQ_TILE, FWD_COLS), F32),
            pltpu.VMEM((SEQ_TILE, HG_W), F32),
            pltpu.VMEM((SEQ_TILE, ATT_QW), BF16),
            pltpu.VMEM((SEQ_TILE, ATT_QW), BF16),
        ] + _hgrn_scratch(),
        compiler_params=pltpu.CompilerParams(
            dimension_semantics=("arbitrary", "arbitrary"), vmem_limit_bytes=VMEM_LIMIT_BYTES),
        name="mix_fwd",
    )(sink, x, cos, sin, ob, qh, vh, kk, kk, kk, vv, vv, vv, g, w, lbp, hgn, wrec, watt, wout)


def _mem_xattn_kernel(h_ref, k_ref, v_ref, g_ref, wq_ref, wo_ref, o_ref, att_s):
    h = h_ref[0]
    n = _rms(h, g_ref[...]).astype(BF16)
    q = (_dot(n, wq_ref[...]) * (MEM_HEAD_DIM ** -0.5 * LOG2E)).astype(BF16)
    for hd in range(MEM_HEADS):
        sl = slice(hd * MEM_HEAD_DIM, (hd + 1) * MEM_HEAD_DIM)
        s = _dot_nt(q[:, sl], k_ref[0, :, sl])
        m = jnp.max(s, axis=-1, keepdims=True)
        p = jnp.exp2(s - m)
        inv = 1.0 / jnp.sum(p, axis=-1, keepdims=True)
        att_s[:, sl] = (_dot(p.astype(BF16), v_ref[0, :, sl]) * inv).astype(BF16)
    o_ref[0] = h + _dot(att_s[...], wo_ref[...])


def _mem_xattn(h, k_mem, v_mem, g, wq, wo):
    b, s, _ = h.shape
    tile = pl.BlockSpec((1, SEQ_TILE, D_MODEL), lambda bi, i: (bi, i, 0))
    kv = pl.BlockSpec((1, MEM_LEN, D_MODEL), lambda bi, i: (bi, 0, 0))
    return pl.pallas_call(
        _mem_xattn_kernel,
        grid=(b, s // SEQ_TILE),
        in_specs=[tile, kv, kv, _const_spec((1, D_MODEL)),
                  _const_spec((D_MODEL, D_MODEL)), _const_spec((D_MODEL, D_MODEL))],
        out_specs=tile,
        out_shape=jax.ShapeDtypeStruct((b, s, D_MODEL), F32),
        scratch_shapes=[pltpu.VMEM((SEQ_TILE, D_MODEL), BF16)],
        compiler_params=pltpu.CompilerParams(
            dimension_semantics=("arbitrary", "arbitrary"), vmem_limit_bytes=VMEM_LIMIT_BYTES),
        name="mem_xattn",
    )(h, k_mem, v_mem, g, wq, wo)


FFN_ROWS = SEQ_TILE + 2 * SUBLANES_F32


def _conv_ffn_kernel(hp_ref, h_ref, hn_ref, g_ref, wu_ref, wg_ref, cw_ref, cb_ref, wd_ref, fn_ref,
                     o_ref):
    ti = pl.program_id(1)
    h = h_ref[0]
    hcat = jnp.concatenate([hp_ref[0], h, hn_ref[0]], axis=0)
    nf = _rms(hcat, g_ref[...])
    n_all = nf.astype(BF16)
    n_main = nf[SUBLANES_F32:SUBLANES_F32 + SEQ_TILE].astype(BF16)
    ridx = lax.broadcasted_iota(jnp.int32, (FFN_ROWS, 1), 0)
    pad_lo = (ridx < SUBLANES_F32) & (ti == 0)
    pad_hi = (ridx >= SUBLANES_F32 + SEQ_TILE) & (ti == pl.num_programs(1) - 1)
    rowmask = jnp.where(pad_lo | pad_hi, 0.0, 1.0)
    main = slice(SUBLANES_F32, SUBLANES_F32 + SEQ_TILE)
    starts = [sum(FF_CHUNKS[:i]) for i in range(len(FF_CHUNKS))]
    col_slices = [slice(c0, c0 + w) for c0, w in zip(starts, FF_CHUNKS)]

    def up(cols):
        return _dot(n_main, wu_ref[:, cols]), _dot(n_all, wg_ref[:, cols]) * rowmask

    acc = jnp.zeros((SEQ_TILE, D_MODEL), F32)
    ahead = up(col_slices[0])
    for i, cols in enumerate(col_slices):
        u, gx = ahead
        if i + 1 < len(col_slices):
            ahead = up(col_slices[i + 1])
        g_prev = pltpu.roll(gx, 1, axis=0)[main]
        g_next = pltpu.roll(gx, FFN_ROWS - 1, axis=0)[main]
        gc = (g_prev * cw_ref[0:1, cols] + gx[main] * cw_ref[1:2, cols]
              + g_next * cw_ref[2:3, cols] + cb_ref[:, cols])
        a = (gc * _sigmoid(gc) * u).astype(BF16)
        acc = acc + _dot(a, wd_ref[cols, :])
    o_ref[0] = _rms(h + acc, fn_ref[...])


def _conv_ffn(h, g, wu, wg, cw, cb, wd, fn):
    b, s, _ = h.shape
    r = SEQ_TILE // SUBLANES_F32
    nblk = s // SUBLANES_F32
    tile = pl.BlockSpec((1, SEQ_TILE, D_MODEL), lambda bi, i: (bi, i, 0))
    prev = pl.BlockSpec((1, SUBLANES_F32, D_MODEL), lambda bi, i: (bi, jnp.maximum(i * r - 1, 0), 0))
    nxt = pl.BlockSpec((1, SUBLANES_F32, D_MODEL),
                       lambda bi, i: (bi, jnp.minimum((i + 1) * r, nblk - 1), 0))
    return pl.pallas_call(
        _conv_ffn_kernel,
        grid=(b, s // SEQ_TILE),
        in_specs=[prev, tile, nxt, _const_spec((1, D_MODEL)),
                  _const_spec((D_MODEL, D_FF)), _const_spec((D_MODEL, D_FF)),
                  _const_spec((3, D_FF)), _const_spec((1, D_FF)),
                  _const_spec((D_FF, D_MODEL)), _const_spec((1, D_MODEL))],
        out_specs=tile,
        out_shape=jax.ShapeDtypeStruct((b, s, D_MODEL), F32),
        compiler_params=pltpu.CompilerParams(
            dimension_semantics=("arbitrary", "arbitrary"), vmem_limit_bytes=VMEM_LIMIT_BYTES),
        name="conv_ffn",
    )(h, h, h, g, wu, wg, cw, cb, wd, fn)


def kernel(x, mem, positions, norm_mix, w_in, lower_bounds, hg_norm, attn_sink, w_br_rec, w_br_att,
           w_mix_out, norm_mem, norm_mem_kv, w_mem_q, w_mem_kv, w_mem_o, norm_ffn, w_up, conv_w,
           conv_b, w_down, final_norm):
    assert x.shape[1] % SEQ_TILE == 0 and x.shape[2] == D_MODEL and w_in.shape[0] == 1
    bf = lambda a: a.astype(BF16)
    row = lambda a: a.reshape(1, -1).astype(F32)

    o = 0
    cols = {}
    for name, width in (("q_r", HG_W), ("fz_f", HG_W), ("fz_b", HG_W), ("i_r", HG_W), ("g_r", HG_W),
                        ("q_a", ATT_QW), ("k_a", ATT_KVW), ("v_a", ATT_KVW),
                        ("gate_r", D_MODEL), ("gate_a", D_MODEL)):
        cols[name] = w_in[0][:, o:o + width]
        o += width
    w_bwd = bf(jnp.concatenate([cols[k] for k in ("q_r", "fz_b", "i_r", "k_a", "v_a")], axis=1))
    w_fwd = bf(jnp.concatenate(
        [cols[k] for k in ("fz_f", "g_r", "q_a", "gate_r", "gate_a")], axis=1))

    cos, sin = _rope_tables(positions)
    g_mix = row(norm_mix[0])

    k_mem, v_mem = _mem_kv(mem, row(norm_mem_kv[0]), bf(w_mem_kv[0]))
    o_bwd, kk, vv, qh, vh = _mix_bwd(x, cos, sin, g_mix, w_bwd, lower_bounds[1].astype(F32))
    h1 = _mix_fwd(attn_sink[0].astype(F32), x, cos, sin, o_bwd, qh, vh, kk, vv, g_mix, w_fwd,
                  lower_bounds[0].astype(F32), row(hg_norm[0]), bf(w_br_rec[0]), bf(w_br_att[0]),
                  bf(w_mix_out[0]))
    h2 = _mem_xattn(h1, k_mem, v_mem, row(norm_mem[0]), bf(w_mem_q[0]), bf(w_mem_o[0]))
    return _conv_ffn(h2, row(norm_ffn[0]), bf(w_up[0][:, :D_FF]), bf(w_up[0][:, D_FF:]),
                     conv_w[0].astype(F32), row(conv_b[0]), bf(w_down[0]), row(final_norm))
```

```python
import functools

import jax
import jax.numpy as jnp
from jax import lax
from jax.experimental import pallas as pl
from jax.experimental.pallas import tpu as pltpu

F32 = jnp.float32
BF16 = jnp.bfloat16

D_MODEL = 1024
MEM_LEN = 256
HG_HEADS = 4
HG_DIM = 128
HG_W = HG_HEADS * HG_DIM
HG_CHUNK = 64
HG_MID = HG_CHUNK // 2
ATT_Q_HEADS = 8
ATT_KV_HEADS = 2
ATT_HEAD_DIM = 64
ATT_QW = ATT_Q_HEADS * ATT_HEAD_DIM
ATT_KVW = ATT_KV_HEADS * ATT_HEAD_DIM
WINDOW = 128
ATT_BLOCK = 128
ROPE_THETA = 500000.0
ROT_DIM = ATT_HEAD_DIM // 4
ROT_HALF = ROT_DIM // 2
MEM_HEADS = 4
MEM_HEAD_DIM = D_MODEL // MEM_HEADS
D_FF = 2816
NORM_EPS = 1e-6
LOG2E = 1.4426950408889634

LANES = 128
SUBLANES_F32 = 8
VMEM_LIMIT_BYTES = 56 * 1024 * 1024

SEQ_TILE = 512
FF_CHUNKS = (768, 768, 768, 512)


def _const_spec(shape):
    nd = len(shape)
    return pl.BlockSpec(shape, lambda *_: (0,) * nd, pipeline_mode=pl.Buffered(1))


def _rms(x, g):
    ms = jnp.mean(x * x, axis=-1, keepdims=True)
    return x * lax.rsqrt(ms + NORM_EPS) * g


def _sigmoid(x):
    return 0.5 * jnp.tanh(0.5 * x) + 0.5


def _dot(a, b):
    return jnp.dot(a, b, preferred_element_type=F32)


def _dot_nt(a, b):
    return lax.dot_general(a, b, (((1,), (1,)), ((), ())), preferred_element_type=F32)


def _dot_tn(a, b):
    return lax.dot_general(a, b, (((0,), (0,)), ((), ())), preferred_element_type=F32)


ROPE_SLOTS = LANES // ROT_DIM
ROPE_ROWS = SEQ_TILE // ROPE_SLOTS


def _rope_kernel(pos_ref, cos_ref, sin_ref):
    lane = lax.broadcasted_iota(jnp.int32, (1, LANES), 1)
    fidx = lane % ROT_HALF
    inv_freq = jnp.zeros((1, LANES), F32)
    for i in range(ROT_HALF):
        inv_freq = jnp.where(fidx == i, 1.0 / (ROPE_THETA ** (2.0 * i / ROT_DIM)), inv_freq)
    ang = pos_ref[...].astype(F32) * inv_freq
    cos_ref[...] = jnp.cos(ang)
    sin_ref[...] = jnp.sin(ang)


def _rope_tables(positions):
    b, s = positions.shape
    nt = s // SEQ_TILE
    p = positions.reshape(b, nt, ROPE_SLOTS, ROPE_ROWS).transpose(0, 1, 3, 2)
    p = jnp.broadcast_to(p[..., None], (b, nt, ROPE_ROWS, ROPE_SLOTS, ROT_DIM))
    p = p.reshape(b * nt * ROPE_ROWS, LANES)
    full = pl.BlockSpec(p.shape, lambda: (0, 0))
    cos, sin = pl.pallas_call(
        _rope_kernel,
        in_specs=[full],
        out_specs=[full, full],
        out_shape=[jax.ShapeDtypeStruct(p.shape, F32)] * 2,
        name="rope_tab",
    )(p)
    return cos.reshape(b, nt * ROPE_ROWS, LANES), sin.reshape(b, nt * ROPE_ROWS, LANES)


def _rope_block(cos_ref, sin_ref, slot):
    lane = lax.broadcasted_iota(jnp.int32, (1, LANES), 1)
    d = lane % ATT_HEAD_DIM
    first = lane < ATT_HEAD_DIM

    def spread(packed):
        sh = (LANES - slot * ROT_DIM) % LANES
        a = pltpu.roll(packed, sh, axis=1) if sh else packed
        b = pltpu.roll(packed, (sh + ATT_HEAD_DIM) % LANES, axis=1)
        return jnp.where(first, a, b)

    cos = spread(cos_ref[0])
    sin = spread(sin_ref[0])
    c = jnp.where(d < ROT_DIM, cos, 1.0)
    s_lo = jnp.where(d < ROT_HALF, -sin, 0.0)
    s_hi = jnp.where((d >= ROT_HALF) & (d < ROT_DIM), sin, 0.0)
    return c, s_lo, s_hi


def _rope_apply(t, c, s_lo, s_hi):
    up = pltpu.roll(t, LANES - ROT_HALF, axis=1)
    dn = pltpu.roll(t, ROT_HALF, axis=1)
    return t * c + up * s_lo + dn * s_hi


def _lower_bound(lbp_ref):
    p = lbp_ref[...]
    m = jnp.max(p, axis=0, keepdims=True)
    e = jnp.exp(p - m)
    return e[0:1, :] / jnp.sum(e, axis=0, keepdims=True)


def _hgrn_gates(fz, lb, scratch):
    _, lf_s, k_s, _ = scratch[:4]
    f = lb + (1.0 - lb) * _sigmoid(fz)
    k_s[...] = 1.0 - f
    lf_s[...] = jnp.log2(f)


HG_GROUPS = HG_CHUNK // SUBLANES_F32
N_CHUNKS = SEQ_TILE // HG_CHUNK


def _hgrn_scratch():
    wide_bf = pltpu.VMEM((SEQ_TILE, HG_W), BF16)
    return [
        pltpu.VMEM((SEQ_TILE, HG_W), F32),
        pltpu.VMEM((SEQ_TILE, HG_W), F32),
        pltpu.VMEM((SEQ_TILE, HG_W), F32),
        wide_bf,
        wide_bf, wide_bf, wide_bf, wide_bf,
        pltpu.VMEM((N_CHUNKS, SUBLANES_F32, HG_W), F32),
        pltpu.VMEM((N_CHUNKS * HG_HEADS, HG_CHUNK, HG_CHUNK), BF16),
        pltpu.VMEM((N_CHUNKS * HG_HEADS, HG_DIM, HG_DIM), F32),
        pltpu.VMEM((N_CHUNKS * HG_HEADS, HG_DIM, HG_DIM), BF16),
        pltpu.VMEM((HG_HEADS, HG_DIM, HG_DIM), F32),
    ]


def _hgrn_stages(scratch, o_ref, reverse):
    return (functools.partial(_hgrn_operands, scratch, reverse),
            functools.partial(_hgrn_local, scratch, reverse),
            functools.partial(_hgrn_recurrence, scratch, reverse),
            functools.partial(_hgrn_outputs, scratch, o_ref))


def _hgrn_operands(scratch, reverse):
    q_s, cs_s, k_s, _, qin_s, kin_s, qst_s, kst_s, dec_s = scratch[:9]
    end_row = 0 if reverse else HG_CHUNK - 1

    srow = lax.broadcasted_iota(jnp.int32, (SUBLANES_F32, HG_W), 0)
    for c in range(N_CHUNKS):
        carry = None
        for j in (range(HG_GROUPS - 1, -1, -1) if reverse else range(HG_GROUPS)):
            r0 = c * HG_CHUNK + j * SUBLANES_F32
            x = cs_s[r0:r0 + SUBLANES_F32, :]
            for s in (1, 2, 4):
                if reverse:
                    x = x + jnp.where(srow < SUBLANES_F32 - s,
                                      pltpu.roll(x, SUBLANES_F32 - s, axis=0), 0.0)
                else:
                    x = x + jnp.where(srow >= s, pltpu.roll(x, s, axis=0), 0.0)
            if carry is not None:
                x = x + carry
            cs_s[r0:r0 + SUBLANES_F32, :] = x
            edge = x[0:1, :] if reverse else x[SUBLANES_F32 - 1:SUBLANES_F32, :]
            carry = jnp.broadcast_to(edge, (SUBLANES_F32, HG_W))

    for c in range(N_CHUNKS):
        rows = slice(c * HG_CHUNK, (c + 1) * HG_CHUNK)
        cs = cs_s[rows, :]
        c_end = cs[end_row:end_row + 1, :]
        c_mid = cs[HG_MID:HG_MID + 1, :]
        q = q_s[rows, :]
        k = k_s[rows, :]
        qin_s[rows, :] = (q * jnp.exp2(cs - c_mid)).astype(BF16)
        kin_s[rows, :] = (k * jnp.exp2(c_mid - cs)).astype(BF16)
        qst_s[rows, :] = (q * jnp.exp2(cs)).astype(BF16)
        kst_s[rows, :] = (k * jnp.exp2(c_end - cs)).astype(BF16)
        dec_s[c] = jnp.broadcast_to(jnp.exp2(c_end), (SUBLANES_F32, HG_W))


def _hgrn_local(scratch, reverse):
    _, _, _, v_s, qin_s, kin_s, _, kst_s, _, a_s, u_s = scratch[:11]
    row = lax.broadcasted_iota(jnp.int32, (HG_CHUNK, HG_CHUNK), 0)
    col = lax.broadcasted_iota(jnp.int32, (HG_CHUNK, HG_CHUNK), 1)
    keep = (col >= row) if reverse else (col <= row)
    for c in range(N_CHUNKS):
        rows = slice(c * HG_CHUNK, (c + 1) * HG_CHUNK)
        for h in range(HG_HEADS):
            sl = slice(h * HG_DIM, (h + 1) * HG_DIM)
            a = _dot_nt(qin_s[rows, sl], kin_s[rows, sl])
            a_s[c * HG_HEADS + h] = jnp.where(keep, a, 0.0).astype(BF16)
            u_s[c * HG_HEADS + h] = _dot_tn(v_s[rows, sl], kst_s[rows, sl])


def _hgrn_recurrence(scratch, reverse):
    dec_s, _, u_s, sall_s, st_s = scratch[8:]
    order = range(N_CHUNKS - 1, -1, -1) if reverse else range(N_CHUNKS)
    for h in range(HG_HEADS):
        sl = slice(h * HG_DIM, (h + 1) * HG_DIM)
        st = st_s[h]
        for c in order:
            sall_s[c * HG_HEADS + h] = st.astype(BF16)
            st = st * dec_s[c, 0:1, sl] + u_s[c * HG_HEADS + h]
        st_s[h] = st


def _hgrn_outputs(scratch, o_ref):
    v_s, qst_s, a_s, sall_s = scratch[3], scratch[6], scratch[9], scratch[11]
    for c in range(N_CHUNKS):
        rows = slice(c * HG_CHUNK, (c + 1) * HG_CHUNK)
        for h in range(HG_HEADS):
            sl = slice(h * HG_DIM, (h + 1) * HG_DIM)
            i = c * HG_HEADS + h
            o_ref[rows, sl] = _dot(a_s[i], v_s[rows, sl]) + _dot_nt(qst_s[rows, sl], sall_s[i])


def _mem_kv_kernel(mem_ref, g_ref, w_ref, k_ref, v_ref):
    n = _rms(mem_ref[0], g_ref[...]).astype(BF16)
    kv = _dot(n, w_ref[...])
    k_ref[0] = kv[:, :D_MODEL].astype(BF16)
    v_ref[0] = kv[:, D_MODEL:].astype(BF16)


def _mem_kv(mem, g, w_kv):
    b = mem.shape[0]
    blk = pl.BlockSpec((1, MEM_LEN, D_MODEL), lambda i: (i, 0, 0))
    return pl.pallas_call(
        _mem_kv_kernel,
        grid=(b,),
        in_specs=[blk, _const_spec((1, D_MODEL)), _const_spec((D_MODEL, 2 * D_MODEL))],
        out_specs=[blk, blk],
        out_shape=[jax.ShapeDtypeStruct((b, MEM_LEN, D_MODEL), BF16)] * 2,
        compiler_params=pltpu.CompilerParams(
            dimension_semantics=("arbitrary",), vmem_limit_bytes=VMEM_LIMIT_BYTES),
        name="mem_kv",
    )(mem, g, w_kv)


BWD_COLS = 3 * HG_W + 2 * ATT_KVW


PK_KK, PK_VV, PK_QH, PK_VH = 0, 4 * LANES, 8 * LANES, 8 * LANES + HG_W
PK_W = PK_VH + HG_W
PK_HALO_W = PK_QH


def _mix_bwd_kernel(x_ref, cos_ref, sin_ref, g_ref, w_ref, lbp_ref,
                    ob_ref, pk_ref,
                    *hg):
    q_s, _, _, v_s = hg[:4]
    st_s = hg[-1]

    @pl.when(pl.program_id(1) == 0)
    def _():
        st_s[...] = jnp.zeros_like(st_s)

    n = _rms(x_ref[0], g_ref[...]).astype(BF16)
    project = lambda lo, hi: _dot(n, w_ref[:, lo:hi])
    operands, local, recurrence, outputs = _hgrn_stages(hg, ob_ref.at[0], reverse=True)

    fz = project(HG_W, 2 * HG_W)
    q_r = project(0, HG_W)
    _hgrn_gates(fz, _lower_bound(lbp_ref), hg)
    v = project(2 * HG_W, 3 * HG_W).astype(BF16)
    q = (q_r * _sigmoid(q_r)) * (HG_DIM ** -0.5)
    q_s[...] = q
    v_s[...] = v
    pk_ref[0, :, PK_QH:PK_QH + HG_W] = q.astype(BF16)
    pk_ref[0, :, PK_VH:PK_VH + HG_W] = v
    kv = project(3 * HG_W, BWD_COLS)
    operands()

    lane = lax.broadcasted_iota(jnp.int32, (1, LANES), 1)
    first = lane < ATT_HEAD_DIM

    def emit_kv(slots):
        for slot in slots:
            rows = slice(slot * ROPE_ROWS, (slot + 1) * ROPE_ROWS)
            k_a = _rope_apply(kv[rows, 0:ATT_KVW], *_rope_block(cos_ref, sin_ref, slot))
            v_a = kv[rows, ATT_KVW:2 * ATT_KVW]
            for src, base in ((k_a, PK_KK), (v_a, PK_VV)):
                sw = pltpu.roll(src, ATT_HEAD_DIM, axis=1)
                blocks = (jnp.where(first, src, 0.0), jnp.where(first, 0.0, sw),
                          jnp.where(first, sw, 0.0), jnp.where(first, 0.0, src))
                for i, blk in enumerate(blocks):
                    pk_ref[0, rows, base + i * LANES:base + (i + 1) * LANES] = blk.astype(BF16)

    local()
    emit_kv(range(0, ROPE_SLOTS // 2))
    recurrence()
    emit_kv(range(ROPE_SLOTS // 2, ROPE_SLOTS))
    outputs()


def _mix_bwd(x, cos, sin, g, w, lbp):
    b, s, _ = x.shape
    nt = s // SEQ_TILE
    rev = lambda bi, i: (bi, nt - 1 - i, 0)
    tile = lambda w_: pl.BlockSpec((1, SEQ_TILE, w_), rev)
    rope = pl.BlockSpec((1, ROPE_ROWS, LANES), rev)
    return pl.pallas_call(
        _mix_bwd_kernel,
        grid=(b, nt),
        in_specs=[
            tile(D_MODEL), rope, rope,
            _const_spec((1, D_MODEL)),
            _const_spec((D_MODEL, BWD_COLS)),
            _const_spec(lbp.shape),
        ],
        out_specs=[tile(HG_W), tile(PK_W)],
        out_shape=[jax.ShapeDtypeStruct((b, s, HG_W), F32),
                   jax.ShapeDtypeStruct((b, s, PK_W), BF16)],
        scratch_shapes=_hgrn_scratch(),
        compiler_params=pltpu.CompilerParams(
            dimension_semantics=("arbitrary", "arbitrary"), vmem_limit_bytes=VMEM_LIMIT_BYTES),
        name="mix_bwd",
    )(x, cos, sin, g, w, lbp)


FWD_COLS = 2 * HG_W + ATT_QW + 2 * D_MODEL
Q_BLOCKS = SEQ_TILE // ATT_BLOCK
KEY_SPAN = 3 * ATT_BLOCK


def _mix_fwd_kernel(sink_ref, x_ref, cos_ref, sin_ref, ob_ref, pk_prev, pk_main, pk_next,
                    g_ref, w_ref, lbp_ref, hgn_ref, wrec_ref, watt_ref, wout_ref,
                    h_ref,
                    proj_s, of_s, qrot_s, att_s, *hg):
    ti = pl.program_id(1)
    q_s, _, _, v_s = hg[:4]
    st_s = hg[-1]

    @pl.when(ti == 0)
    def _():
        st_s[...] = jnp.zeros_like(st_s)

    x = x_ref[0]
    n = _rms(x, g_ref[...]).astype(BF16)

    def project(lo, hi):
        proj_s[:, lo:hi] = _dot(n, w_ref[:, lo:hi])

    g0 = 2 * HG_W + ATT_QW
    operands, local, recurrence, outputs = _hgrn_stages(hg, of_s, reverse=False)

    def recurrent_branch():
        o = of_s[...] + ob_ref[0]
        g_r = proj_s[:, HG_W:2 * HG_W]
        gate = g_r * _sigmoid(g_r)
        hgn = hgn_ref[...]
        parts = []
        for h in range(HG_HEADS):
            sl = slice(h * HG_DIM, (h + 1) * HG_DIM)
            oh = o[:, sl]
            ms = jnp.mean(oh * oh, axis=-1, keepdims=True)
            parts.append((oh * lax.rsqrt(ms + NORM_EPS) * hgn[:, sl] * gate[:, sl]).astype(BF16))
        return jnp.concatenate(parts, axis=1)

    def rotate_queries():
        q0 = 2 * HG_W
        for slot in range(ROPE_SLOTS):
            rws = slice(slot * ROPE_ROWS, (slot + 1) * ROPE_ROWS)
            tables = _rope_block(cos_ref, sin_ref, slot)
            for j in range(ATT_QW // LANES):
                t = proj_s[rws, q0 + j * LANES:q0 + (j + 1) * LANES]
                qrot_s[rws, j * LANES:(j + 1) * LANES] = (
                    _rope_apply(t, *tables) * (ATT_HEAD_DIM ** -0.5 * LOG2E)).astype(BF16)

    project(0, HG_W)
    project(2 * HG_W, g0)
    q_s[...] = pk_main[0, :, PK_QH:PK_QH + HG_W].astype(F32)
    v_s[...] = pk_main[0, :, PK_VH:PK_VH + HG_W]
    _hgrn_gates(proj_s[:, 0:HG_W], _lower_bound(lbp_ref), hg)
    project(HG_W, 2 * HG_W)
    operands()
    project(g0, g0 + D_MODEL)
    rotate_queries()

    kk_cols = slice(PK_KK, PK_KK + 4 * LANES)
    vv_cols = slice(PK_VV, PK_VV + 4 * LANES)
    kk_all = jnp.concatenate(
        [pk_prev[0, :, kk_cols], pk_main[0, :, kk_cols], pk_next[0, :, kk_cols]], axis=0)
    vv_all = jnp.concatenate(
        [pk_prev[0, :, vv_cols], pk_main[0, :, vv_cols], pk_next[0, :, vv_cols]], axis=0)
    r_i = lax.broadcasted_iota(jnp.int32, (ATT_BLOCK, KEY_SPAN), 0)
    c_i = lax.broadcasted_iota(jnp.int32, (ATT_BLOCK, KEY_SPAN), 1)
    off = c_i - r_i
    lane = lax.broadcasted_iota(jnp.int32, (1, LANES), 1)
    first = lane < ATT_HEAD_DIM
    seq_len = pl.num_programs(1) * SEQ_TILE
    seg_of_row = lax.broadcasted_iota(jnp.int32, (2 * KEY_SPAN, LANES), 0) // KEY_SPAN
    seg_of_lane = lax.broadcasted_iota(jnp.int32, (2 * KEY_SPAN, LANES), 1) // ATT_HEAD_DIM
    ones_bd = jnp.where(seg_of_row == seg_of_lane, 1.0, 0.0).astype(BF16)
    units = [(j, g) for j in range(Q_BLOCKS) for g in range(ATT_KV_HEADS)]

    def scores(j, g):
        rows = slice(j * ATT_BLOCK, (j + 1) * ATT_BLOCK)
        keys = slice(j * ATT_BLOCK, j * ATT_BLOCK + KEY_SPAN)
        ca = slice(2 * g * LANES, (2 * g + 1) * LANES)
        cb = slice((2 * g + 1) * LANES, (2 * g + 2) * LANES)
        lhs = jnp.concatenate([qrot_s[rows, ca], qrot_s[rows, cb]], axis=0)
        k_bd = jnp.concatenate([kk_all[keys, ca], kk_all[keys, cb]], axis=0)
        return _dot_nt(lhs, k_bd)

    pending = [scores(*units[0])]

    def attend(u):
        j, g = units[u]
        s = pending.pop()
        if u + 1 < len(units):
            pending.append(scores(*units[u + 1]))
        kstart = ti * SEQ_TILE + (j - 1) * ATT_BLOCK
        kpos = c_i + kstart
        valid = (off >= 0) & (off <= 2 * WINDOW) & (kpos >= 0) & (kpos < seq_len)
        bias = jnp.where(valid, 0.0, -jnp.inf)
        rows = slice(j * ATT_BLOCK, (j + 1) * ATT_BLOCK)
        keys = slice(j * ATT_BLOCK, j * ATT_BLOCK + KEY_SPAN)
        ca = slice(2 * g * LANES, (2 * g + 1) * LANES)
        cb = slice((2 * g + 1) * LANES, (2 * g + 2) * LANES)
        v_bd = jnp.concatenate([vv_all[keys, ca], vv_all[keys, cb]], axis=0)
        v_ext = jnp.concatenate([v_bd, ones_bd], axis=1)
        p_rows, sink_rows = [], []
        for half in range(2):
            p_seg, sink_seg = [], []
            for seg in range(2):
                head = 4 * g + 2 * half + seg
                ss = s[half * ATT_BLOCK:(half + 1) * ATT_BLOCK,
                       seg * KEY_SPAN:(seg + 1) * KEY_SPAN] + bias
                sk = sink_ref[head] * LOG2E
                m = jnp.maximum(jnp.max(ss, axis=-1, keepdims=True), sk)
                p_seg.append(jnp.exp2(ss - m).astype(BF16))
                sink_seg.append(jnp.exp2(sk - m))
            p_rows.append(jnp.concatenate(p_seg, axis=1))
            sink_rows.append(jnp.where(first, sink_seg[0], sink_seg[1]))
        pv = _dot(jnp.concatenate(p_rows, axis=0), v_ext)
        den = pv[:, LANES:] + jnp.concatenate(sink_rows, axis=0)
        out = pv[:, :LANES] / den
        att_s[rows, ca] = out[:ATT_BLOCK].astype(BF16)
        att_s[rows, cb] = out[ATT_BLOCK:].astype(BF16)

    quarter = len(units) // 4
    local()
    for u in range(0, quarter):
        attend(u)
    recurrence()
    for u in range(quarter, 2 * quarter):
        attend(u)
    outputs()
    for u in range(2 * quarter, 3 * quarter):
        attend(u)
    rec_in = recurrent_branch()
    project(g0 + D_MODEL, g0 + 2 * D_MODEL)
    for u in range(3 * quarter, len(units)):
        attend(u)
    y_rec = _dot(rec_in, wrec_ref[...])
    y_att = _dot(att_s[...], watt_ref[...])

    merged = (_sigmoid(proj_s[:, g0:g0 + D_MODEL]) * y_rec
              + _sigmoid(proj_s[:, g0 + D_MODEL:g0 + 2 * D_MODEL]) * y_att)
    h_ref[0] = x + _dot(merged.astype(BF16), wout_ref[...])


def _mix_fwd(sink, x, cos, sin, ob, pk, g, w, lbp, hgn, wrec, watt, wout):
    b, s, _ = x.shape
    nt = s // SEQ_TILE
    r = SEQ_TILE // ATT_BLOCK
    nblk = s // ATT_BLOCK
    main = lambda bi, i: (bi, i, 0)
    prev = lambda bi, i: (bi, jnp.maximum(i * r - 1, 0), 0)
    nxt = lambda bi, i: (bi, jnp.minimum((i + 1) * r, nblk - 1), 0)
    halo = lambda imap: pl.BlockSpec((1, ATT_BLOCK, PK_HALO_W), imap)
    tile = lambda w_: pl.BlockSpec((1, SEQ_TILE, w_), main)
    rope = pl.BlockSpec((1, ROPE_ROWS, LANES), main)
    return pl.pallas_call(
        _mix_fwd_kernel,
        grid=(b, nt),
        in_specs=[
            pl.BlockSpec(memory_space=pltpu.SMEM),
            tile(D_MODEL), rope, rope, tile(HG_W),
            halo(prev), tile(PK_W), halo(nxt),
            _const_spec((1, D_MODEL)),
            _const_spec((D_MODEL, FWD_COLS)),
            _const_spec(lbp.shape),
            _const_spec((1, HG_W)),
            _const_spec((HG_W, D_MODEL)),
            _const_spec((ATT_QW, D_MODEL)),
            _const_spec((D_MODEL, D_MODEL)),
        ],
        out_specs=tile(D_MODEL),
        out_shape=jax.ShapeDtypeStruct((b, s, D_MODEL), F32),
        scratch_shapes=[
            pltpu.VMEM((SEQ_TILE, FWD_COLS), F32),
            pltpu.VMEM((SEQ_TILE, HG_W), F32),
            pltpu.VMEM((SEQ_TILE, ATT_QW), BF16),
            pltpu.VMEM((SEQ_TILE, ATT_QW), BF16),
        ] + _hgrn_scratch(),
        compiler_params=pltpu.CompilerParams(
            dimension_semantics=("arbitrary", "arbitrary"), vmem_limit_bytes=VMEM_LIMIT_BYTES),
        name="mix_fwd",
    )(sink, x, cos, sin, ob, pk, pk, pk, g, w, lbp, hgn, wrec, watt, wout)


def _mem_xattn_kernel(h_ref, k_ref, v_ref, g_ref, wq_ref, wo_ref, o_ref, att_s):
    h = h_ref[0]
    n = _rms(h, g_ref[...]).astype(BF16)
    q = (_dot(n, wq_ref[...]) * (MEM_HEAD_DIM ** -0.5 * LOG2E)).astype(BF16)
    head = lambda hd: slice(hd * MEM_HEAD_DIM, (hd + 1) * MEM_HEAD_DIM)
    scores = lambda hd: _dot_nt(q[:, head(hd)], k_ref[0, :, head(hd)])
    ahead = scores(0)
    for hd in range(MEM_HEADS):
        sl = head(hd)
        s = ahead
        if hd + 1 < MEM_HEADS:
            ahead = scores(hd + 1)
        m = jnp.max(s, axis=-1, keepdims=True)
        p = jnp.exp2(s - m)
        inv = 1.0 / jnp.sum(p, axis=-1, keepdims=True)
        att_s[:, sl] = (_dot(p.astype(BF16), v_ref[0, :, sl]) * inv).astype(BF16)
    o_ref[0] = h + _dot(att_s[...], wo_ref[...])


def _mem_xattn(h, k_mem, v_mem, g, wq, wo):
    b, s, _ = h.shape
    tile = pl.BlockSpec((1, SEQ_TILE, D_MODEL), lambda bi, i: (bi, i, 0))
    kv = pl.BlockSpec((1, MEM_LEN, D_MODEL), lambda bi, i: (bi, 0, 0))
    return pl.pallas_call(
        _mem_xattn_kernel,
        grid=(b, s // SEQ_TILE),
        in_specs=[tile, kv, kv, _const_spec((1, D_MODEL)),
                  _const_spec((D_MODEL, D_MODEL)), _const_spec((D_MODEL, D_MODEL))],
        out_specs=tile,
        out_shape=jax.ShapeDtypeStruct((b, s, D_MODEL), F32),
        scratch_shapes=[pltpu.VMEM((SEQ_TILE, D_MODEL), BF16)],
        compiler_params=pltpu.CompilerParams(
            dimension_semantics=("arbitrary", "arbitrary"), vmem_limit_bytes=VMEM_LIMIT_BYTES),
        name="mem_xattn",
    )(h, k_mem, v_mem, g, wq, wo)


FFN_ROWS = SEQ_TILE + 2 * SUBLANES_F32


def _conv_ffn_kernel(hp_ref, h_ref, hn_ref, g_ref, wu_ref, wg_ref, cw_ref, cb_ref, wd_ref, fn_ref,
                     o_ref):
    ti = pl.program_id(1)
    h = h_ref[0]
    hcat = jnp.concatenate([hp_ref[0], h, hn_ref[0]], axis=0)
    nf = _rms(hcat, g_ref[...])
    n_all = nf.astype(BF16)
    n_main = nf[SUBLANES_F32:SUBLANES_F32 + SEQ_TILE].astype(BF16)
    ridx = lax.broadcasted_iota(jnp.int32, (FFN_ROWS, 1), 0)
    pad_lo = (ridx < SUBLANES_F32) & (ti == 0)
    pad_hi = (ridx >= SUBLANES_F32 + SEQ_TILE) & (ti == pl.num_programs(1) - 1)
    rowmask = jnp.where(pad_lo | pad_hi, 0.0, 1.0)
    main = slice(SUBLANES_F32, SUBLANES_F32 + SEQ_TILE)
    starts = [sum(FF_CHUNKS[:i]) for i in range(len(FF_CHUNKS))]
    col_slices = [slice(c0, c0 + w) for c0, w in zip(starts, FF_CHUNKS)]

    def up(cols):
        return _dot(n_main, wu_ref[:, cols]), _dot(n_all, wg_ref[:, cols]) * rowmask

    acc = jnp.zeros((SEQ_TILE, D_MODEL), F32)
    ahead = up(col_slices[0])
    for i, cols in enumerate(col_slices):
        u, gx = ahead
        if i + 1 < len(col_slices):
            ahead = up(col_slices[i + 1])
        g_prev = pltpu.roll(gx, 1, axis=0)[main]
        g_next = pltpu.roll(gx, FFN_ROWS - 1, axis=0)[main]
        gc = (g_prev * cw_ref[0:1, cols] + gx[main] * cw_ref[1:2, cols]
              + g_next * cw_ref[2:3, cols] + cb_ref[:, cols])
        a = (gc * _sigmoid(gc) * u).astype(BF16)
        acc = acc + _dot(a, wd_ref[cols, :])
    o_ref[0] = _rms(h + acc, fn_ref[...])


def _conv_ffn(h, g, wu, wg, cw, cb, wd, fn):
    b, s, _ = h.shape
    r = SEQ_TILE // SUBLANES_F32
    nblk = s // SUBLANES_F32
    tile = pl.BlockSpec((1, SEQ_TILE, D_MODEL), lambda bi, i: (bi, i, 0))
    prev = pl.BlockSpec((1, SUBLANES_F32, D_MODEL), lambda bi, i: (bi, jnp.maximum(i * r - 1, 0), 0))
    nxt = pl.BlockSpec((1, SUBLANES_F32, D_MODEL),
                       lambda bi, i: (bi, jnp.minimum((i + 1) * r, nblk - 1), 0))
    return pl.pallas_call(
        _conv_ffn_kernel,
        grid=(b, s // SEQ_TILE),
        in_specs=[prev, tile, nxt, _const_spec((1, D_MODEL)),
                  _const_spec((D_MODEL, D_FF)), _const_spec((D_MODEL, D_FF)),
                  _const_spec((3, D_FF)), _const_spec((1, D_FF)),
                  _const_spec((D_FF, D_MODEL)), _const_spec((1, D_MODEL))],
        out_specs=tile,
        out_shape=jax.ShapeDtypeStruct((b, s, D_MODEL), F32),
        compiler_params=pltpu.CompilerParams(
            dimension_semantics=("arbitrary", "arbitrary"), vmem_limit_bytes=VMEM_LIMIT_BYTES),
        name="conv_ffn",
    )(h, h, h, g, wu, wg, cw, cb, wd, fn)


def kernel(x, mem, positions, norm_mix, w_in, lower_bounds, hg_norm, attn_sink, w_br_rec, w_br_att,
           w_mix_out, norm_mem, norm_mem_kv, w_mem_q, w_mem_kv, w_mem_o, norm_ffn, w_up, conv_w,
           conv_b, w_down, final_norm):
    assert x.shape[1] % SEQ_TILE == 0 and x.shape[2] == D_MODEL and w_in.shape[0] == 1
    bf = lambda a: a.astype(BF16)
    row = lambda a: a.reshape(1, -1).astype(F32)

    o = 0
    cols = {}
    for name, width in (("q_r", HG_W), ("fz_f", HG_W), ("fz_b", HG_W), ("i_r", HG_W), ("g_r", HG_W),
                        ("q_a", ATT_QW), ("k_a", ATT_KVW), ("v_a", ATT_KVW),
                        ("gate_r", D_MODEL), ("gate_a", D_MODEL)):
        cols[name] = w_in[0][:, o:o + width]
        o += width
    w_bwd = bf(jnp.concatenate([cols[k] for k in ("q_r", "fz_b", "i_r", "k_a", "v_a")], axis=1))
    w_fwd = bf(jnp.concatenate(
        [cols[k] for k in ("fz_f", "g_r", "q_a", "gate_r", "gate_a")], axis=1))

    cos, sin = _rope_tables(positions)
    g_mix = row(norm_mix[0])

    k_mem, v_mem = _mem_kv(mem, row(norm_mem_kv[0]), bf(w_mem_kv[0]))
    o_bwd, pk = _mix_bwd(x, cos, sin, g_mix, w_bwd, lower_bounds[1].astype(F32))
    h1 = _mix_fwd(attn_sink[0].astype(F32), x, cos, sin, o_bwd, pk, g_mix, w_fwd,
                  lower_bounds[0].astype(F32), row(hg_norm[0]), bf(w_br_rec[0]), bf(w_br_att[0]),
                  bf(w_mix_out[0]))
    h2 = _mem_xattn(h1, k_mem, v_mem, row(norm_mem[0]), bf(w_mem_q[0]), bf(w_mem_o[0]))
    return _conv_ffn(h2, row(norm_ffn[0]), bf(w_up[0][:, :D_FF]), bf(w_up[0][:, D_FF:]),
                     conv_w[0].astype(F32), row(conv_b[0]), bf(w_down[0]), row(final_norm))
```

```python
import functools

import jax
import jax.numpy as jnp
from jax import lax
from jax.experimental import pallas as pl
from jax.experimental.pallas import tpu as pltpu

F32 = jnp.float32
BF16 = jnp.bfloat16

D_MODEL = 1024
MEM_LEN = 256
HG_HEADS = 4
HG_DIM = 128
HG_W = HG_HEADS * HG_DIM
HG_CHUNK = 64
HG_MID = HG_CHUNK // 2
ATT_Q_HEADS = 8
ATT_KV_HEADS = 2
ATT_HEAD_DIM = 64
ATT_QW = ATT_Q_HEADS * ATT_HEAD_DIM
ATT_KVW = ATT_KV_HEADS * ATT_HEAD_DIM
WINDOW = 128
ATT_BLOCK = 128
ROPE_THETA = 500000.0
ROT_DIM = ATT_HEAD_DIM // 4
ROT_HALF = ROT_DIM // 2
MEM_HEADS = 4
MEM_HEAD_DIM = D_MODEL // MEM_HEADS
D_FF = 2816
NORM_EPS = 1e-6
LOG2E = 1.4426950408889634

LANES = 128
SUBLANES_F32 = 8
VMEM_LIMIT_BYTES = 56 * 1024 * 1024

SEQ_TILE = 512
FF_CHUNKS = (768, 768, 768, 512)


def _const_spec(shape):
    nd = len(shape)
    return pl.BlockSpec(shape, lambda *_: (0,) * nd, pipeline_mode=pl.Buffered(1))


def _rms(x, g):
    ms = jnp.mean(x * x, axis=-1, keepdims=True)
    return x * lax.rsqrt(ms + NORM_EPS) * g


def _sigmoid(x):
    return 0.5 * jnp.tanh(0.5 * x) + 0.5


def _dot(a, b):
    return jnp.dot(a, b, preferred_element_type=F32)


def _dot_nt(a, b):
    return lax.dot_general(a, b, (((1,), (1,)), ((), ())), preferred_element_type=F32)


def _dot_tn(a, b):
    return lax.dot_general(a, b, (((0,), (0,)), ((), ())), preferred_element_type=F32)


ROPE_SLOTS = LANES // ROT_DIM
ROPE_ROWS = SEQ_TILE // ROPE_SLOTS


def _rope_kernel(pos_ref, cos_ref, sin_ref):
    lane = lax.broadcasted_iota(jnp.int32, (1, LANES), 1)
    fidx = lane % ROT_HALF
    inv_freq = jnp.zeros((1, LANES), F32)
    for i in range(ROT_HALF):
        inv_freq = jnp.where(fidx == i, 1.0 / (ROPE_THETA ** (2.0 * i / ROT_DIM)), inv_freq)
    ang = pos_ref[...].astype(F32) * inv_freq
    cos_ref[...] = jnp.cos(ang)
    sin_ref[...] = jnp.sin(ang)


def _rope_tables(positions):
    b, s = positions.shape
    nt = s // SEQ_TILE
    p = positions.reshape(b, nt, ROPE_SLOTS, ROPE_ROWS).transpose(0, 1, 3, 2)
    p = jnp.broadcast_to(p[..., None], (b, nt, ROPE_ROWS, ROPE_SLOTS, ROT_DIM))
    p = p.reshape(b * nt * ROPE_ROWS, LANES)
    full = pl.BlockSpec(p.shape, lambda: (0, 0))
    cos, sin = pl.pallas_call(
        _rope_kernel,
        in_specs=[full],
        out_specs=[full, full],
        out_shape=[jax.ShapeDtypeStruct(p.shape, F32)] * 2,
        name="rope_tab",
    )(p)
    return cos.reshape(b, nt * ROPE_ROWS, LANES), sin.reshape(b, nt * ROPE_ROWS, LANES)


def _rope_block(cos_ref, sin_ref, slot):
    lane = lax.broadcasted_iota(jnp.int32, (1, LANES), 1)
    d = lane % ATT_HEAD_DIM
    first = lane < ATT_HEAD_DIM

    def spread(packed):
        sh = (LANES - slot * ROT_DIM) % LANES
        a = pltpu.roll(packed, sh, axis=1) if sh else packed
        b = pltpu.roll(packed, (sh + ATT_HEAD_DIM) % LANES, axis=1)
        return jnp.where(first, a, b)

    cos = spread(cos_ref[0])
    sin = spread(sin_ref[0])
    c = jnp.where(d < ROT_DIM, cos, 1.0)
    s_lo = jnp.where(d < ROT_HALF, -sin, 0.0)
    s_hi = jnp.where((d >= ROT_HALF) & (d < ROT_DIM), sin, 0.0)
    return c, s_lo, s_hi


def _rope_apply(t, c, s_lo, s_hi):
    up = pltpu.roll(t, LANES - ROT_HALF, axis=1)
    dn = pltpu.roll(t, ROT_HALF, axis=1)
    return t * c + up * s_lo + dn * s_hi


def _lower_bound(lbp_ref):
    p = lbp_ref[...]
    m = jnp.max(p, axis=0, keepdims=True)
    e = jnp.exp(p - m)
    return e[0:1, :] / jnp.sum(e, axis=0, keepdims=True)


def _hgrn_gates(fz, lb, scratch):
    _, lf_s, k_s, _ = scratch[:4]
    f = lb + (1.0 - lb) * _sigmoid(fz)
    k_s[...] = 1.0 - f
    lf_s[...] = jnp.log2(f)


HG_GROUPS = HG_CHUNK // SUBLANES_F32
N_CHUNKS = SEQ_TILE // HG_CHUNK
HG_PAIRS = HG_HEADS // 2
HG_PAIR_W = 2 * HG_DIM


def _hgrn_scratch():
    wide_bf = pltpu.VMEM((SEQ_TILE, HG_W), BF16)
    units = N_CHUNKS * HG_PAIRS
    return [
        pltpu.VMEM((SEQ_TILE, HG_W), F32),
        pltpu.VMEM((SEQ_TILE, HG_W), F32),
        pltpu.VMEM((SEQ_TILE, HG_W), F32),
        wide_bf,
        wide_bf, wide_bf, wide_bf, wide_bf,
        pltpu.VMEM((N_CHUNKS, SUBLANES_F32, HG_W), F32),
        pltpu.VMEM((units, HG_CHUNK, 2 * HG_CHUNK), BF16),
        pltpu.VMEM((units, HG_DIM, HG_PAIR_W), F32),
        pltpu.VMEM((units, HG_DIM, HG_PAIR_W), BF16),
        pltpu.VMEM((HG_PAIRS, HG_DIM, HG_PAIR_W), F32),
    ]


def _block_diag(x):
    w = x.shape[1] // 2
    z = jnp.zeros((x.shape[0], w), x.dtype)
    return jnp.concatenate([jnp.concatenate([x[:, :w], z], axis=1),
                            jnp.concatenate([z, x[:, w:]], axis=1)], axis=0)


def _hgrn_stages(scratch, o_ref, reverse):
    return (functools.partial(_hgrn_operands, scratch, reverse),
            functools.partial(_hgrn_local, scratch, reverse),
            functools.partial(_hgrn_recurrence, scratch, reverse),
            functools.partial(_hgrn_outputs, scratch, o_ref))


def _hgrn_operands(scratch, reverse):
    q_s, cs_s, k_s, _, qin_s, kin_s, qst_s, kst_s, dec_s = scratch[:9]
    end_row = 0 if reverse else HG_CHUNK - 1

    srow = lax.broadcasted_iota(jnp.int32, (SUBLANES_F32, HG_W), 0)
    for c in range(N_CHUNKS):
        carry = None
        for j in (range(HG_GROUPS - 1, -1, -1) if reverse else range(HG_GROUPS)):
            r0 = c * HG_CHUNK + j * SUBLANES_F32
            x = cs_s[r0:r0 + SUBLANES_F32, :]
            for s in (1, 2, 4):
                if reverse:
                    x = x + jnp.where(srow < SUBLANES_F32 - s,
                                      pltpu.roll(x, SUBLANES_F32 - s, axis=0), 0.0)
                else:
                    x = x + jnp.where(srow >= s, pltpu.roll(x, s, axis=0), 0.0)
            if carry is not None:
                x = x + carry
            cs_s[r0:r0 + SUBLANES_F32, :] = x
            edge = x[0:1, :] if reverse else x[SUBLANES_F32 - 1:SUBLANES_F32, :]
            carry = jnp.broadcast_to(edge, (SUBLANES_F32, HG_W))

    for c in range(N_CHUNKS):
        rows = slice(c * HG_CHUNK, (c + 1) * HG_CHUNK)
        cs = cs_s[rows, :]
        c_end = cs[end_row:end_row + 1, :]
        c_mid = cs[HG_MID:HG_MID + 1, :]
        q = q_s[rows, :]
        k = k_s[rows, :]
        qin_s[rows, :] = (q * jnp.exp2(cs - c_mid)).astype(BF16)
        kin_s[rows, :] = (k * jnp.exp2(c_mid - cs)).astype(BF16)
        qst_s[rows, :] = (q * jnp.exp2(cs)).astype(BF16)
        kst_s[rows, :] = (k * jnp.exp2(c_end - cs)).astype(BF16)
        dec_s[c] = jnp.broadcast_to(jnp.exp2(c_end), (SUBLANES_F32, HG_W))


def _hgrn_local(scratch, reverse):
    _, _, _, v_s, qin_s, kin_s, _, kst_s, _, a_s, u_s = scratch[:11]
    row = lax.broadcasted_iota(jnp.int32, (HG_CHUNK, 2 * HG_CHUNK), 0)
    col = lax.broadcasted_iota(jnp.int32, (HG_CHUNK, 2 * HG_CHUNK), 1) % HG_CHUNK
    keep = (col >= row) if reverse else (col <= row)
    for c in range(N_CHUNKS):
        rows = slice(c * HG_CHUNK, (c + 1) * HG_CHUNK)
        for p in range(HG_PAIRS):
            sl = slice(p * HG_PAIR_W, (p + 1) * HG_PAIR_W)
            i = c * HG_PAIRS + p
            a = _dot_nt(qin_s[rows, sl], _block_diag(kin_s[rows, sl]))
            a_s[i] = jnp.where(keep, a, 0.0).astype(BF16)
            v = v_s[rows, sl]
            v_rows = jnp.concatenate([v[:, :HG_DIM], v[:, HG_DIM:]], axis=0)
            u_s[i] = _dot_tn(v_rows, _block_diag(kst_s[rows, sl]))


def _hgrn_recurrence(scratch, reverse):
    dec_s, _, u_s, sall_s, st_s = scratch[8:]
    order = range(N_CHUNKS - 1, -1, -1) if reverse else range(N_CHUNKS)
    for p in range(HG_PAIRS):
        sl = slice(p * HG_PAIR_W, (p + 1) * HG_PAIR_W)
        st = st_s[p]
        for c in order:
            sall_s[c * HG_PAIRS + p] = st.astype(BF16)
            st = st * dec_s[c, 0:1, sl] + u_s[c * HG_PAIRS + p]
        st_s[p] = st


def _hgrn_outputs(scratch, o_ref):
    v_s, qst_s, a_s, sall_s = scratch[3], scratch[6], scratch[9], scratch[11]
    for c in range(N_CHUNKS):
        rows = slice(c * HG_CHUNK, (c + 1) * HG_CHUNK)
        for p in range(HG_PAIRS):
            sl = slice(p * HG_PAIR_W, (p + 1) * HG_PAIR_W)
            i = c * HG_PAIRS + p
            o_ref[rows, sl] = (_dot(a_s[i], _block_diag(v_s[rows, sl]))
                               + _dot_nt(qst_s[rows, sl], _block_diag(sall_s[i])))


def _mem_kv_kernel(mem_ref, g_ref, w_ref, k_ref, v_ref):
    n = _rms(mem_ref[0], g_ref[...]).astype(BF16)
    kv = _dot(n, w_ref[...])
    k_ref[0] = kv[:, :D_MODEL].astype(BF16)
    v_ref[0] = kv[:, D_MODEL:].astype(BF16)


def _mem_kv(mem, g, w_kv):
    b = mem.shape[0]
    blk = pl.BlockSpec((1, MEM_LEN, D_MODEL), lambda i: (i, 0, 0))
    return pl.pallas_call(
        _mem_kv_kernel,
        grid=(b,),
        in_specs=[blk, _const_spec((1, D_MODEL)), _const_spec((D_MODEL, 2 * D_MODEL))],
        out_specs=[blk, blk],
        out_shape=[jax.ShapeDtypeStruct((b, MEM_LEN, D_MODEL), BF16)] * 2,
        compiler_params=pltpu.CompilerParams(
            dimension_semantics=("arbitrary",), vmem_limit_bytes=VMEM_LIMIT_BYTES),
        name="mem_kv",
    )(mem, g, w_kv)


BWD_COLS = 3 * HG_W + 2 * ATT_KVW


PK_KK, PK_VV, PK_QH, PK_VH = 0, 4 * LANES, 8 * LANES, 8 * LANES + HG_W
PK_W = PK_VH + HG_W
PK_HALO_W = PK_QH


def _mix_bwd_kernel(x_ref, cos_ref, sin_ref, g_ref, w_ref, lbp_ref,
                    ob_ref, pk_ref,
                    *hg):
    q_s, _, _, v_s = hg[:4]
    st_s = hg[-1]

    @pl.when(pl.program_id(1) == 0)
    def _():
        st_s[...] = jnp.zeros_like(st_s)

    n = _rms(x_ref[0], g_ref[...]).astype(BF16)
    project = lambda lo, hi: _dot(n, w_ref[:, lo:hi])
    operands, local, recurrence, outputs = _hgrn_stages(hg, ob_ref.at[0], reverse=True)

    fz = project(HG_W, 2 * HG_W)
    q_r = project(0, HG_W)
    _hgrn_gates(fz, _lower_bound(lbp_ref), hg)
    v = project(2 * HG_W, 3 * HG_W).astype(BF16)
    q = (q_r * _sigmoid(q_r)) * (HG_DIM ** -0.5)
    q_s[...] = q
    v_s[...] = v
    pk_ref[0, :, PK_QH:PK_QH + HG_W] = q.astype(BF16)
    pk_ref[0, :, PK_VH:PK_VH + HG_W] = v
    kv = project(3 * HG_W, BWD_COLS)
    operands()

    lane = lax.broadcasted_iota(jnp.int32, (1, LANES), 1)
    first = lane < ATT_HEAD_DIM

    def emit_kv(slots):
        for slot in slots:
            rows = slice(slot * ROPE_ROWS, (slot + 1) * ROPE_ROWS)
            k_a = _rope_apply(kv[rows, 0:ATT_KVW], *_rope_block(cos_ref, sin_ref, slot))
            v_a = kv[rows, ATT_KVW:2 * ATT_KVW]
            for src, base in ((k_a, PK_KK), (v_a, PK_VV)):
                sw = pltpu.roll(src, ATT_HEAD_DIM, axis=1)
                blocks = (jnp.where(first, src, 0.0), jnp.where(first, 0.0, sw),
                          jnp.where(first, sw, 0.0), jnp.where(first, 0.0, src))
                for i, blk in enumerate(blocks):
                    pk_ref[0, rows, base + i * LANES:base + (i + 1) * LANES] = blk.astype(BF16)

    local()
    emit_kv(range(0, ROPE_SLOTS // 2))
    recurrence()
    emit_kv(range(ROPE_SLOTS // 2, ROPE_SLOTS))
    outputs()


def _mix_bwd(x, cos, sin, g, w, lbp):
    b, s, _ = x.shape
    nt = s // SEQ_TILE
    rev = lambda bi, i: (bi, nt - 1 - i, 0)
    tile = lambda w_: pl.BlockSpec((1, SEQ_TILE, w_), rev)
    rope = pl.BlockSpec((1, ROPE_ROWS, LANES), rev)
    return pl.pallas_call(
        _mix_bwd_kernel,
        grid=(b, nt),
        in_specs=[
            tile(D_MODEL), rope, rope,
            _const_spec((1, D_MODEL)),
            _const_spec((D_MODEL, BWD_COLS)),
            _const_spec(lbp.shape),
        ],
        out_specs=[tile(HG_W), tile(PK_W)],
        out_shape=[jax.ShapeDtypeStruct((b, s, HG_W), F32),
                   jax.ShapeDtypeStruct((b, s, PK_W), BF16)],
        scratch_shapes=_hgrn_scratch(),
        compiler_params=pltpu.CompilerParams(
            dimension_semantics=("arbitrary", "arbitrary"), vmem_limit_bytes=VMEM_LIMIT_BYTES),
        name="mix_bwd",
    )(x, cos, sin, g, w, lbp)


FWD_COLS = 2 * HG_W + ATT_QW + 2 * D_MODEL
Q_BLOCKS = SEQ_TILE // ATT_BLOCK
KEY_SPAN = 3 * ATT_BLOCK


def _mix_fwd_kernel(sink_ref, x_ref, cos_ref, sin_ref, ob_ref, pk_prev, pk_main, pk_next,
                    g_ref, w_ref, lbp_ref, hgn_ref, wrec_ref, watt_ref, wout_ref,
                    h_ref,
                    proj_s, of_s, qrot_s, att_s, *hg):
    ti = pl.program_id(1)
    q_s, _, _, v_s = hg[:4]
    st_s = hg[-1]

    @pl.when(ti == 0)
    def _():
        st_s[...] = jnp.zeros_like(st_s)

    x = x_ref[0]
    n = _rms(x, g_ref[...]).astype(BF16)

    def project(lo, hi):
        proj_s[:, lo:hi] = _dot(n, w_ref[:, lo:hi])

    g0 = 2 * HG_W + ATT_QW
    operands, local, recurrence, outputs = _hgrn_stages(hg, of_s, reverse=False)

    def recurrent_branch():
        o = of_s[...] + ob_ref[0]
        g_r = proj_s[:, HG_W:2 * HG_W]
        gate = g_r * _sigmoid(g_r)
        hgn = hgn_ref[...]
        parts = []
        for h in range(HG_HEADS):
            sl = slice(h * HG_DIM, (h + 1) * HG_DIM)
            oh = o[:, sl]
            ms = jnp.mean(oh * oh, axis=-1, keepdims=True)
            parts.append((oh * lax.rsqrt(ms + NORM_EPS) * hgn[:, sl] * gate[:, sl]).astype(BF16))
        return jnp.concatenate(parts, axis=1)

    def rotate_queries():
        q0 = 2 * HG_W
        for slot in range(ROPE_SLOTS):
            rws = slice(slot * ROPE_ROWS, (slot + 1) * ROPE_ROWS)
            tables = _rope_block(cos_ref, sin_ref, slot)
            for j in range(ATT_QW // LANES):
                t = proj_s[rws, q0 + j * LANES:q0 + (j + 1) * LANES]
                qrot_s[rws, j * LANES:(j + 1) * LANES] = (
                    _rope_apply(t, *tables) * (ATT_HEAD_DIM ** -0.5 * LOG2E)).astype(BF16)

    project(0, HG_W)
    project(2 * HG_W, g0)
    q_s[...] = pk_main[0, :, PK_QH:PK_QH + HG_W].astype(F32)
    v_s[...] = pk_main[0, :, PK_VH:PK_VH + HG_W]
    _hgrn_gates(proj_s[:, 0:HG_W], _lower_bound(lbp_ref), hg)
    project(HG_W, 2 * HG_W)
    operands()
    project(g0, g0 + D_MODEL)
    rotate_queries()

    kk_cols = slice(PK_KK, PK_KK + 4 * LANES)
    vv_cols = slice(PK_VV, PK_VV + 4 * LANES)
    kk_all = jnp.concatenate(
        [pk_prev[0, :, kk_cols], pk_main[0, :, kk_cols], pk_next[0, :, kk_cols]], axis=0)
    vv_all = jnp.concatenate(
        [pk_prev[0, :, vv_cols], pk_main[0, :, vv_cols], pk_next[0, :, vv_cols]], axis=0)
    r_i = lax.broadcasted_iota(jnp.int32, (ATT_BLOCK, KEY_SPAN), 0)
    c_i = lax.broadcasted_iota(jnp.int32, (ATT_BLOCK, KEY_SPAN), 1)
    off = c_i - r_i
    lane = lax.broadcasted_iota(jnp.int32, (1, LANES), 1)
    first = lane < ATT_HEAD_DIM
    seq_len = pl.num_programs(1) * SEQ_TILE
    seg_of_row = lax.broadcasted_iota(jnp.int32, (2 * KEY_SPAN, LANES), 0) // KEY_SPAN
    seg_of_lane = lax.broadcasted_iota(jnp.int32, (2 * KEY_SPAN, LANES), 1) // ATT_HEAD_DIM
    ones_bd = jnp.where(seg_of_row == seg_of_lane, 1.0, 0.0).astype(BF16)
    units = [(j, g) for j in range(Q_BLOCKS) for g in range(ATT_KV_HEADS)]

    def scores(j, g):
        rows = slice(j * ATT_BLOCK, (j + 1) * ATT_BLOCK)
        keys = slice(j * ATT_BLOCK, j * ATT_BLOCK + KEY_SPAN)
        ca = slice(2 * g * LANES, (2 * g + 1) * LANES)
        cb = slice((2 * g + 1) * LANES, (2 * g + 2) * LANES)
        lhs = jnp.concatenate([qrot_s[rows, ca], qrot_s[rows, cb]], axis=0)
        k_bd = jnp.concatenate([kk_all[keys, ca], kk_all[keys, cb]], axis=0)
        return _dot_nt(lhs, k_bd)

    pending = [scores(*units[0])]

    def attend(u):
        j, g = units[u]
        s = pending.pop()
        if u + 1 < len(units):
            pending.append(scores(*units[u + 1]))
        kstart = ti * SEQ_TILE + (j - 1) * ATT_BLOCK
        kpos = c_i + kstart
        valid = (off >= 0) & (off <= 2 * WINDOW) & (kpos >= 0) & (kpos < seq_len)
        bias = jnp.where(valid, 0.0, -jnp.inf)
        rows = slice(j * ATT_BLOCK, (j + 1) * ATT_BLOCK)
        keys = slice(j * ATT_BLOCK, j * ATT_BLOCK + KEY_SPAN)
        ca = slice(2 * g * LANES, (2 * g + 1) * LANES)
        cb = slice((2 * g + 1) * LANES, (2 * g + 2) * LANES)
        v_bd = jnp.concatenate([vv_all[keys, ca], vv_all[keys, cb]], axis=0)
        v_ext = jnp.concatenate([v_bd, ones_bd], axis=1)
        p_rows, sink_rows = [], []
        for half in range(2):
            p_seg, sink_seg = [], []
            for seg in range(2):
                head = 4 * g + 2 * half + seg
                ss = s[half * ATT_BLOCK:(half + 1) * ATT_BLOCK,
                       seg * KEY_SPAN:(seg + 1) * KEY_SPAN] + bias
                sk = sink_ref[head] * LOG2E
                m = jnp.maximum(jnp.max(ss, axis=-1, keepdims=True), sk)
                p_seg.append(jnp.exp2(ss - m).astype(BF16))
                sink_seg.append(jnp.exp2(sk - m))
            p_rows.append(jnp.concatenate(p_seg, axis=1))
            sink_rows.append(jnp.where(first, sink_seg[0], sink_seg[1]))
        pv = _dot(jnp.concatenate(p_rows, axis=0), v_ext)
        den = pv[:, LANES:] + jnp.concatenate(sink_rows, axis=0)
        out = pv[:, :LANES] / den
        att_s[rows, ca] = out[:ATT_BLOCK].astype(BF16)
        att_s[rows, cb] = out[ATT_BLOCK:].astype(BF16)

    quarter = len(units) // 4
    local()
    for u in range(0, quarter):
        attend(u)
    recurrence()
    for u in range(quarter, 2 * quarter):
        attend(u)
    outputs()
    for u in range(2 * quarter, 3 * quarter):
        attend(u)
    rec_in = recurrent_branch()
    project(g0 + D_MODEL, g0 + 2 * D_MODEL)
    for u in range(3 * quarter, len(units)):
        attend(u)
    y_rec = _dot(rec_in, wrec_ref[...])
    y_att = _dot(att_s[...], watt_ref[...])

    merged = (_sigmoid(proj_s[:, g0:g0 + D_MODEL]) * y_rec
              + _sigmoid(proj_s[:, g0 + D_MODEL:g0 + 2 * D_MODEL]) * y_att)
    h_ref[0] = x + _dot(merged.astype(BF16), wout_ref[...])


def _mix_fwd(sink, x, cos, sin, ob, pk, g, w, lbp, hgn, wrec, watt, wout):
    b, s, _ = x.shape
    nt = s // SEQ_TILE
    r = SEQ_TILE // ATT_BLOCK
    nblk = s // ATT_BLOCK
    main = lambda bi, i: (bi, i, 0)
    prev = lambda bi, i: (bi, jnp.maximum(i * r - 1, 0), 0)
    nxt = lambda bi, i: (bi, jnp.minimum((i + 1) * r, nblk - 1), 0)
    halo = lambda imap: pl.BlockSpec((1, ATT_BLOCK, PK_HALO_W), imap)
    tile = lambda w_: pl.BlockSpec((1, SEQ_TILE, w_), main)
    rope = pl.BlockSpec((1, ROPE_ROWS, LANES), main)
    return pl.pallas_call(
        _mix_fwd_kernel,
        grid=(b, nt),
        in_specs=[
            pl.BlockSpec(memory_space=pltpu.SMEM),
            tile(D_MODEL), rope, rope, tile(HG_W),
            halo(prev), tile(PK_W), halo(nxt),
            _const_spec((1, D_MODEL)),
            _const_spec((D_MODEL, FWD_COLS)),
            _const_spec(lbp.shape),
            _const_spec((1, HG_W)),
            _const_spec((HG_W, D_MODEL)),
            _const_spec((ATT_QW, D_MODEL)),
            _const_spec((D_MODEL, D_MODEL)),
        ],
        out_specs=tile(D_MODEL),
        out_shape=jax.ShapeDtypeStruct((b, s, D_MODEL), F32),
        scratch_shapes=[
            pltpu.VMEM((SEQ_TILE, FWD_COLS), F32),
            pltpu.VMEM((SEQ_TILE, HG_W), F32),
            pltpu.VMEM((SEQ_TILE, ATT_QW), BF16),
            pltpu.VMEM((SEQ_TILE, ATT_QW), BF16),
        ] + _hgrn_scratch(),
        compiler_params=pltpu.CompilerParams(
            dimension_semantics=("arbitrary", "arbitrary"), vmem_limit_bytes=VMEM_LIMIT_BYTES),
        name="mix_fwd",
    )(sink, x, cos, sin, ob, pk, pk, pk, g, w, lbp, hgn, wrec, watt, wout)


def _mem_xattn_kernel(h_ref, k_ref, v_ref, g_ref, wq_ref, wo_ref, o_ref, att_s):
    h = h_ref[0]
    n = _rms(h, g_ref[...]).astype(BF16)
    q = (_dot(n, wq_ref[...]) * (MEM_HEAD_DIM ** -0.5 * LOG2E)).astype(BF16)
    head = lambda hd: slice(hd * MEM_HEAD_DIM, (hd + 1) * MEM_HEAD_DIM)
    scores = lambda hd: _dot_nt(q[:, head(hd)], k_ref[0, :, head(hd)])
    ahead = scores(0)
    for hd in range(MEM_HEADS):
        sl = head(hd)
        s = ahead
        if hd + 1 < MEM_HEADS:
            ahead = scores(hd + 1)
        m = jnp.max(s, axis=-1, keepdims=True)
        p = jnp.exp2(s - m)
        inv = 1.0 / jnp.sum(p, axis=-1, keepdims=True)
        att_s[:, sl] = (_dot(p.astype(BF16), v_ref[0, :, sl]) * inv).astype(BF16)
    o_ref[0] = h + _dot(att_s[...], wo_ref[...])


def _mem_xattn(h, k_mem, v_mem, g, wq, wo):
    b, s, _ = h.shape
    tile = pl.BlockSpec((1, SEQ_TILE, D_MODEL), lambda bi, i: (bi, i, 0))
    kv = pl.BlockSpec((1, MEM_LEN, D_MODEL), lambda bi, i: (bi, 0, 0))
    return pl.pallas_call(
        _mem_xattn_kernel,
        grid=(b, s // SEQ_TILE),
        in_specs=[tile, kv, kv, _const_spec((1, D_MODEL)),
                  _const_spec((D_MODEL, D_MODEL)), _const_spec((D_MODEL, D_MODEL))],
        out_specs=tile,
        out_shape=jax.ShapeDtypeStruct((b, s, D_MODEL), F32),
        scratch_shapes=[pltpu.VMEM((SEQ_TILE, D_MODEL), BF16)],
        compiler_params=pltpu.CompilerParams(
            dimension_semantics=("arbitrary", "arbitrary"), vmem_limit_bytes=VMEM_LIMIT_BYTES),
        name="mem_xattn",
    )(h, k_mem, v_mem, g, wq, wo)


FFN_ROWS = SEQ_TILE + 2 * SUBLANES_F32


def _conv_ffn_kernel(hp_ref, h_ref, hn_ref, g_ref, wu_ref, wg_ref, cw_ref, cb_ref, wd_ref, fn_ref,
                     o_ref):
    ti = pl.program_id(1)
    h = h_ref[0]
    hcat = jnp.concatenate([hp_ref[0], h, hn_ref[0]], axis=0)
    nf = _rms(hcat, g_ref[...])
    n_all = nf.astype(BF16)
    n_main = nf[SUBLANES_F32:SUBLANES_F32 + SEQ_TILE].astype(BF16)
    ridx = lax.broadcasted_iota(jnp.int32, (FFN_ROWS, 1), 0)
    pad_lo = (ridx < SUBLANES_F32) & (ti == 0)
    pad_hi = (ridx >= SUBLANES_F32 + SEQ_TILE) & (ti == pl.num_programs(1) - 1)
    rowmask = jnp.where(pad_lo | pad_hi, 0.0, 1.0)
    main = slice(SUBLANES_F32, SUBLANES_F32 + SEQ_TILE)
    starts = [sum(FF_CHUNKS[:i]) for i in range(len(FF_CHUNKS))]
    col_slices = [slice(c0, c0 + w) for c0, w in zip(starts, FF_CHUNKS)]

    def up(cols):
        return _dot(n_main, wu_ref[:, cols]), _dot(n_all, wg_ref[:, cols]) * rowmask

    acc = jnp.zeros((SEQ_TILE, D_MODEL), F32)
    ahead = up(col_slices[0])
    for i, cols in enumerate(col_slices):
        u, gx = ahead
        if i + 1 < len(col_slices):
            ahead = up(col_slices[i + 1])
        g_prev = pltpu.roll(gx, 1, axis=0)[main]
        g_next = pltpu.roll(gx, FFN_ROWS - 1, axis=0)[main]
        gc = (g_prev * cw_ref[0:1, cols] + gx[main] * cw_ref[1:2, cols]
              + g_next * cw_ref[2:3, cols] + cb_ref[:, cols])
        a = (gc * _sigmoid(gc) * u).astype(BF16)
        acc = acc + _dot(a, wd_ref[cols, :])
    o_ref[0] = _rms(h + acc, fn_ref[...])


def _conv_ffn(h, g, wu, wg, cw, cb, wd, fn):
    b, s, _ = h.shape
    r = SEQ_TILE // SUBLANES_F32
    nblk = s // SUBLANES_F32
    tile = pl.BlockSpec((1, SEQ_TILE, D_MODEL), lambda bi, i: (bi, i, 0))
    prev = pl.BlockSpec((1, SUBLANES_F32, D_MODEL), lambda bi, i: (bi, jnp.maximum(i * r - 1, 0), 0))
    nxt = pl.BlockSpec((1, SUBLANES_F32, D_MODEL),
                       lambda bi, i: (bi, jnp.minimum((i + 1) * r, nblk - 1), 0))
    return pl.pallas_call(
        _conv_ffn_kernel,
        grid=(b, s // SEQ_TILE),
        in_specs=[prev, tile, nxt, _const_spec((1, D_MODEL)),
                  _const_spec((D_MODEL, D_FF)), _const_spec((D_MODEL, D_FF)),
                  _const_spec((3, D_FF)), _const_spec((1, D_FF)),
                  _const_spec((D_FF, D_MODEL)), _const_spec((1, D_MODEL))],
        out_specs=tile,
        out_shape=jax.ShapeDtypeStruct((b, s, D_MODEL), F32),
        compiler_params=pltpu.CompilerParams(
            dimension_semantics=("arbitrary", "arbitrary"), vmem_limit_bytes=VMEM_LIMIT_BYTES),
        name="conv_ffn",
    )(h, h, h, g, wu, wg, cw, cb, wd, fn)


def kernel(x, mem, positions, norm_mix, w_in, lower_bounds, hg_norm, attn_sink, w_br_rec, w_br_att,
           w_mix_out, norm_mem, norm_mem_kv, w_mem_q, w_mem_kv, w_mem_o, norm_ffn, w_up, conv_w,
           conv_b, w_down, final_norm):
    assert x.shape[1] % SEQ_TILE == 0 and x.shape[2] == D_MODEL and w_in.shape[0] == 1
    bf = lambda a: a.astype(BF16)
    row = lambda a: a.reshape(1, -1).astype(F32)

    o = 0
    cols = {}
    for name, width in (("q_r", HG_W), ("fz_f", HG_W), ("fz_b", HG_W), ("i_r", HG_W), ("g_r", HG_W),
                        ("q_a", ATT_QW), ("k_a", ATT_KVW), ("v_a", ATT_KVW),
                        ("gate_r", D_MODEL), ("gate_a", D_MODEL)):
        cols[name] = w_in[0][:, o:o + width]
        o += width
    w_bwd = bf(jnp.concatenate([cols[k] for k in ("q_r", "fz_b", "i_r", "k_a", "v_a")], axis=1))
    w_fwd = bf(jnp.concatenate(
        [cols[k] for k in ("fz_f", "g_r", "q_a", "gate_r", "gate_a")], axis=1))

    cos, sin = _rope_tables(positions)
    g_mix = row(norm_mix[0])

    k_mem, v_mem = _mem_kv(mem, row(norm_mem_kv[0]), bf(w_mem_kv[0]))
    o_bwd, pk = _mix_bwd(x, cos, sin, g_mix, w_bwd, lower_bounds[1].astype(F32))
    h1 = _mix_fwd(attn_sink[0].astype(F32), x, cos, sin, o_bwd, pk, g_mix, w_fwd,
                  lower_bounds[0].astype(F32), row(hg_norm[0]), bf(w_br_rec[0]), bf(w_br_att[0]),
                  bf(w_mix_out[0]))
    h2 = _mem_xattn(h1, k_mem, v_mem, row(norm_mem[0]), bf(w_mem_q[0]), bf(w_mem_o[0]))
    return _conv_ffn(h2, row(norm_ffn[0]), bf(w_up[0][:, :D_FF]), bf(w_up[0][:, D_FF:]),
                     conv_w[0].astype(F32), row(conv_b[0]), bf(w_down[0]), row(final_norm))
```

```python
import functools

import jax
import jax.numpy as jnp
from jax import lax
from jax.experimental import pallas as pl
from jax.experimental.pallas import tpu as pltpu

F32 = jnp.float32
BF16 = jnp.bfloat16

D_MODEL = 1024
MEM_LEN = 256
HG_HEADS = 4
HG_DIM = 128
HG_W = HG_HEADS * HG_DIM
HG_CHUNK = 128
HG_MID = HG_CHUNK // 2
ATT_Q_HEADS = 8
ATT_KV_HEADS = 2
ATT_HEAD_DIM = 64
ATT_QW = ATT_Q_HEADS * ATT_HEAD_DIM
ATT_KVW = ATT_KV_HEADS * ATT_HEAD_DIM
WINDOW = 128
ATT_BLOCK = 128
ROPE_THETA = 500000.0
ROT_DIM = ATT_HEAD_DIM // 4
ROT_HALF = ROT_DIM // 2
MEM_HEADS = 4
MEM_HEAD_DIM = D_MODEL // MEM_HEADS
D_FF = 2816
NORM_EPS = 1e-6
LOG2E = 1.4426950408889634

LANES = 128
SUBLANES_F32 = 8
VMEM_LIMIT_BYTES = 56 * 1024 * 1024

SEQ_TILE = 512
FF_CHUNKS = (1024, 1024, 768)


def _const_spec(shape):
    nd = len(shape)
    return pl.BlockSpec(shape, lambda *_: (0,) * nd, pipeline_mode=pl.Buffered(1))


def _rms(x, g):
    ms = jnp.mean(x * x, axis=-1, keepdims=True)
    return x * lax.rsqrt(ms + NORM_EPS) * g


def _sigmoid(x):
    return 0.5 * jnp.tanh(0.5 * x) + 0.5


def _dot(a, b):
    return jnp.dot(a, b, preferred_element_type=F32)


def _dot_nt(a, b):
    return lax.dot_general(a, b, (((1,), (1,)), ((), ())), preferred_element_type=F32)


def _dot_tn(a, b):
    return lax.dot_general(a, b, (((0,), (0,)), ((), ())), preferred_element_type=F32)


ROPE_SLOTS = LANES // ROT_DIM
ROPE_ROWS = SEQ_TILE // ROPE_SLOTS


def _rope_kernel(pos_ref, cos_ref, sin_ref):
    lane = lax.broadcasted_iota(jnp.int32, (1, LANES), 1)
    fidx = lane % ROT_HALF
    inv_freq = jnp.zeros((1, LANES), F32)
    for i in range(ROT_HALF):
        inv_freq = jnp.where(fidx == i, 1.0 / (ROPE_THETA ** (2.0 * i / ROT_DIM)), inv_freq)
    ang = pos_ref[...].astype(F32) * inv_freq
    cos_ref[...] = jnp.cos(ang)
    sin_ref[...] = jnp.sin(ang)


def _rope_tables(positions):
    b, s = positions.shape
    nt = s // SEQ_TILE
    p = positions.reshape(b, nt, ROPE_SLOTS, ROPE_ROWS).transpose(0, 1, 3, 2)
    p = jnp.broadcast_to(p[..., None], (b, nt, ROPE_ROWS, ROPE_SLOTS, ROT_DIM))
    p = p.reshape(b * nt * ROPE_ROWS, LANES)
    full = pl.BlockSpec(p.shape, lambda: (0, 0))
    cos, sin = pl.pallas_call(
        _rope_kernel,
        in_specs=[full],
        out_specs=[full, full],
        out_shape=[jax.ShapeDtypeStruct(p.shape, F32)] * 2,
        name="rope_tab",
    )(p)
    return cos.reshape(b, nt * ROPE_ROWS, LANES), sin.reshape(b, nt * ROPE_ROWS, LANES)


def _rope_block(cos_ref, sin_ref, slot):
    lane = lax.broadcasted_iota(jnp.int32, (1, LANES), 1)
    d = lane % ATT_HEAD_DIM
    first = lane < ATT_HEAD_DIM

    def spread(packed):
        sh = (LANES - slot * ROT_DIM) % LANES
        a = pltpu.roll(packed, sh, axis=1) if sh else packed
        b = pltpu.roll(packed, (sh + ATT_HEAD_DIM) % LANES, axis=1)
        return jnp.where(first, a, b)

    cos = spread(cos_ref[0])
    sin = spread(sin_ref[0])
    c = jnp.where(d < ROT_DIM, cos, 1.0)
    s_lo = jnp.where(d < ROT_HALF, -sin, 0.0)
    s_hi = jnp.where((d >= ROT_HALF) & (d < ROT_DIM), sin, 0.0)
    return c, s_lo, s_hi


def _rope_apply(t, c, s_lo, s_hi):
    up = pltpu.roll(t, LANES - ROT_HALF, axis=1)
    dn = pltpu.roll(t, ROT_HALF, axis=1)
    return t * c + up * s_lo + dn * s_hi


def _lower_bound(lbp_ref):
    p = lbp_ref[...]
    m = jnp.max(p, axis=0, keepdims=True)
    e = jnp.exp(p - m)
    return e[0:1, :] / jnp.sum(e, axis=0, keepdims=True)


def _hgrn_gates(fz, lb, scratch):
    _, lf_s, k_s, _ = scratch[:4]
    f = lb + (1.0 - lb) * _sigmoid(fz)
    k_s[...] = 1.0 - f
    lf_s[...] = jnp.log2(f)


HG_GROUPS = HG_CHUNK // SUBLANES_F32
N_CHUNKS = SEQ_TILE // HG_CHUNK
HG_PAIRS = HG_HEADS // 2
HG_PAIR_W = 2 * HG_DIM


def _hgrn_scratch():
    wide_bf = pltpu.VMEM((SEQ_TILE, HG_W), BF16)
    units = N_CHUNKS * HG_PAIRS
    return [
        pltpu.VMEM((SEQ_TILE, HG_W), F32),
        pltpu.VMEM((SEQ_TILE, HG_W), F32),
        pltpu.VMEM((SEQ_TILE, HG_W), F32),
        wide_bf,
        wide_bf, wide_bf, wide_bf, wide_bf,
        pltpu.VMEM((N_CHUNKS, SUBLANES_F32, HG_W), F32),
        pltpu.VMEM((units, HG_CHUNK, 2 * HG_CHUNK), BF16),
        pltpu.VMEM((units, HG_DIM, HG_PAIR_W), F32),
        pltpu.VMEM((units, HG_DIM, HG_PAIR_W), BF16),
        pltpu.VMEM((HG_PAIRS, HG_DIM, HG_PAIR_W), F32),
    ]


def _block_diag(x):
    w = x.shape[1] // 2
    z = jnp.zeros((x.shape[0], w), x.dtype)
    return jnp.concatenate([jnp.concatenate([x[:, :w], z], axis=1),
                            jnp.concatenate([z, x[:, w:]], axis=1)], axis=0)


def _hgrn_stages(scratch, o_ref, reverse):
    return (functools.partial(_hgrn_operands, scratch, reverse),
            functools.partial(_hgrn_local, scratch, reverse),
            functools.partial(_hgrn_recurrence, scratch, reverse),
            functools.partial(_hgrn_outputs, scratch, o_ref))


def _hgrn_operands(scratch, reverse):
    q_s, cs_s, k_s, _, qin_s, kin_s, qst_s, kst_s, dec_s = scratch[:9]
    end_row = 0 if reverse else HG_CHUNK - 1

    srow = lax.broadcasted_iota(jnp.int32, (SUBLANES_F32, HG_W), 0)
    for c in range(N_CHUNKS):
        carry = None
        for j in (range(HG_GROUPS - 1, -1, -1) if reverse else range(HG_GROUPS)):
            r0 = c * HG_CHUNK + j * SUBLANES_F32
            x = cs_s[r0:r0 + SUBLANES_F32, :]
            for s in (1, 2, 4):
                if reverse:
                    x = x + jnp.where(srow < SUBLANES_F32 - s,
                                      pltpu.roll(x, SUBLANES_F32 - s, axis=0), 0.0)
                else:
                    x = x + jnp.where(srow >= s, pltpu.roll(x, s, axis=0), 0.0)
            if carry is not None:
                x = x + carry
            cs_s[r0:r0 + SUBLANES_F32, :] = x
            edge = x[0:1, :] if reverse else x[SUBLANES_F32 - 1:SUBLANES_F32, :]
            carry = jnp.broadcast_to(edge, (SUBLANES_F32, HG_W))

    for c in range(N_CHUNKS):
        rows = slice(c * HG_CHUNK, (c + 1) * HG_CHUNK)
        cs = cs_s[rows, :]
        c_end = cs[end_row:end_row + 1, :]
        c_mid = cs[HG_MID:HG_MID + 1, :]
        q = q_s[rows, :]
        k = k_s[rows, :]
        qin_s[rows, :] = (q * jnp.exp2(cs - c_mid)).astype(BF16)
        kin_s[rows, :] = (k * jnp.exp2(c_mid - cs)).astype(BF16)
        qst_s[rows, :] = (q * jnp.exp2(cs)).astype(BF16)
        kst_s[rows, :] = (k * jnp.exp2(c_end - cs)).astype(BF16)
        dec_s[c] = jnp.broadcast_to(jnp.exp2(c_end), (SUBLANES_F32, HG_W))


def _hgrn_local(scratch, reverse):
    _, _, _, v_s, qin_s, kin_s, _, kst_s, _, a_s, u_s = scratch[:11]
    row = lax.broadcasted_iota(jnp.int32, (HG_CHUNK, 2 * HG_CHUNK), 0)
    col = lax.broadcasted_iota(jnp.int32, (HG_CHUNK, 2 * HG_CHUNK), 1) % HG_CHUNK
    keep = (col >= row) if reverse else (col <= row)
    for c in range(N_CHUNKS):
        rows = slice(c * HG_CHUNK, (c + 1) * HG_CHUNK)
        for p in range(HG_PAIRS):
            sl = slice(p * HG_PAIR_W, (p + 1) * HG_PAIR_W)
            i = c * HG_PAIRS + p
            a = _dot_nt(qin_s[rows, sl], _block_diag(kin_s[rows, sl]))
            a_s[i] = jnp.where(keep, a, 0.0).astype(BF16)
            v = v_s[rows, sl]
            v_rows = jnp.concatenate([v[:, :HG_DIM], v[:, HG_DIM:]], axis=0)
            u_s[i] = _dot_tn(v_rows, _block_diag(kst_s[rows, sl]))


def _hgrn_recurrence(scratch, reverse):
    dec_s, _, u_s, sall_s, st_s = scratch[8:]
    order = range(N_CHUNKS - 1, -1, -1) if reverse else range(N_CHUNKS)
    for p in range(HG_PAIRS):
        sl = slice(p * HG_PAIR_W, (p + 1) * HG_PAIR_W)
        st = st_s[p]
        for c in order:
            sall_s[c * HG_PAIRS + p] = st.astype(BF16)
            st = st * dec_s[c, 0:1, sl] + u_s[c * HG_PAIRS + p]
        st_s[p] = st


def _hgrn_outputs(scratch, o_ref):
    v_s, qst_s, a_s, sall_s = scratch[3], scratch[6], scratch[9], scratch[11]
    for c in range(N_CHUNKS):
        rows = slice(c * HG_CHUNK, (c + 1) * HG_CHUNK)
        for p in range(HG_PAIRS):
            sl = slice(p * HG_PAIR_W, (p + 1) * HG_PAIR_W)
            i = c * HG_PAIRS + p
            o_ref[rows, sl] = (_dot(a_s[i], _block_diag(v_s[rows, sl]))
                               + _dot_nt(qst_s[rows, sl], _block_diag(sall_s[i])))


BWD_COLS = 3 * HG_W + 2 * ATT_KVW


PK_KK, PK_VV, PK_QH, PK_VH = 0, 4 * LANES, 8 * LANES, 8 * LANES + HG_W
PK_W = PK_VH + HG_W
PK_HALO_W = PK_QH


def _mix_bwd_kernel(x_ref, cos_ref, sin_ref, g_ref, w_ref, lbp_ref,
                    ob_ref, pk_ref,
                    *hg):
    q_s, _, _, v_s = hg[:4]
    st_s = hg[-1]

    @pl.when(pl.program_id(1) == 0)
    def _():
        st_s[...] = jnp.zeros_like(st_s)

    n = _rms(x_ref[0], g_ref[...]).astype(BF16)
    project = lambda lo, hi: _dot(n, w_ref[:, lo:hi])
    operands, local, recurrence, outputs = _hgrn_stages(hg, ob_ref.at[0], reverse=True)

    fz = project(HG_W, 2 * HG_W)
    q_r = project(0, HG_W)
    _hgrn_gates(fz, _lower_bound(lbp_ref), hg)
    v = project(2 * HG_W, 3 * HG_W).astype(BF16)
    q = (q_r * _sigmoid(q_r)) * (HG_DIM ** -0.5)
    q_s[...] = q
    v_s[...] = v
    pk_ref[0, :, PK_QH:PK_QH + HG_W] = q.astype(BF16)
    pk_ref[0, :, PK_VH:PK_VH + HG_W] = v
    kv = project(3 * HG_W, BWD_COLS)
    operands()

    lane = lax.broadcasted_iota(jnp.int32, (1, LANES), 1)
    first = lane < ATT_HEAD_DIM

    def emit_kv(slots):
        for slot in slots:
            rows = slice(slot * ROPE_ROWS, (slot + 1) * ROPE_ROWS)
            k_a = _rope_apply(kv[rows, 0:ATT_KVW], *_rope_block(cos_ref, sin_ref, slot))
            v_a = kv[rows, ATT_KVW:2 * ATT_KVW]
            for src, base in ((k_a, PK_KK), (v_a, PK_VV)):
                sw = pltpu.roll(src, ATT_HEAD_DIM, axis=1)
                blocks = (jnp.where(first, src, 0.0), jnp.where(first, 0.0, sw),
                          jnp.where(first, sw, 0.0), jnp.where(first, 0.0, src))
                for i, blk in enumerate(blocks):
                    pk_ref[0, rows, base + i * LANES:base + (i + 1) * LANES] = blk.astype(BF16)

    local()
    emit_kv(range(0, ROPE_SLOTS // 2))
    recurrence()
    emit_kv(range(ROPE_SLOTS // 2, ROPE_SLOTS))
    outputs()


def _mix_bwd(x, cos, sin, g, w, lbp):
    b, s, _ = x.shape
    nt = s // SEQ_TILE
    rev = lambda bi, i: (bi, nt - 1 - i, 0)
    tile = lambda w_: pl.BlockSpec((1, SEQ_TILE, w_), rev)
    rope = pl.BlockSpec((1, ROPE_ROWS, LANES), rev)
    return pl.pallas_call(
        _mix_bwd_kernel,
        grid=(b, nt),
        in_specs=[
            tile(D_MODEL), rope, rope,
            _const_spec((1, D_MODEL)),
            _const_spec((D_MODEL, BWD_COLS)),
            _const_spec(lbp.shape),
        ],
        out_specs=[tile(HG_W), tile(PK_W)],
        out_shape=[jax.ShapeDtypeStruct((b, s, HG_W), F32),
                   jax.ShapeDtypeStruct((b, s, PK_W), BF16)],
        scratch_shapes=_hgrn_scratch(),
        compiler_params=pltpu.CompilerParams(
            dimension_semantics=("arbitrary", "arbitrary"), vmem_limit_bytes=VMEM_LIMIT_BYTES),
        name="mix_bwd",
    )(x, cos, sin, g, w, lbp)


FWD_COLS = 2 * HG_W + ATT_QW + 2 * D_MODEL
Q_BLOCKS = SEQ_TILE // ATT_BLOCK
KEY_SPAN = 3 * ATT_BLOCK


def _mix_fwd_kernel(sink_ref, x_ref, cos_ref, sin_ref, ob_ref, pk_prev, pk_main, pk_next,
                    g_ref, w_ref, lbp_ref, hgn_ref, wrec_ref, watt_ref, wout_ref,
                    h_ref,
                    proj_s, of_s, qrot_s, att_s, *hg):
    ti = pl.program_id(1)
    q_s, _, _, v_s = hg[:4]
    st_s = hg[-1]

    @pl.when(ti == 0)
    def _():
        st_s[...] = jnp.zeros_like(st_s)

    x = x_ref[0]
    n = _rms(x, g_ref[...]).astype(BF16)

    def project(lo, hi):
        proj_s[:, lo:hi] = _dot(n, w_ref[:, lo:hi])

    g0 = 2 * HG_W + ATT_QW
    operands, local, recurrence, outputs = _hgrn_stages(hg, of_s, reverse=False)

    def recurrent_branch():
        o = of_s[...] + ob_ref[0]
        g_r = proj_s[:, HG_W:2 * HG_W]
        gate = g_r * _sigmoid(g_r)
        hgn = hgn_ref[...]
        parts = []
        for h in range(HG_HEADS):
            sl = slice(h * HG_DIM, (h + 1) * HG_DIM)
            oh = o[:, sl]
            ms = jnp.mean(oh * oh, axis=-1, keepdims=True)
            parts.append((oh * lax.rsqrt(ms + NORM_EPS) * hgn[:, sl] * gate[:, sl]).astype(BF16))
        return jnp.concatenate(parts, axis=1)

    def rotate_queries():
        q0 = 2 * HG_W
        for slot in range(ROPE_SLOTS):
            rws = slice(slot * ROPE_ROWS, (slot + 1) * ROPE_ROWS)
            tables = _rope_block(cos_ref, sin_ref, slot)
            for j in range(ATT_QW // LANES):
                t = proj_s[rws, q0 + j * LANES:q0 + (j + 1) * LANES]
                qrot_s[rws, j * LANES:(j + 1) * LANES] = (
                    _rope_apply(t, *tables) * (ATT_HEAD_DIM ** -0.5 * LOG2E)).astype(BF16)

    project(0, HG_W)
    project(2 * HG_W, g0)
    q_s[...] = pk_main[0, :, PK_QH:PK_QH + HG_W].astype(F32)
    v_s[...] = pk_main[0, :, PK_VH:PK_VH + HG_W]
    _hgrn_gates(proj_s[:, 0:HG_W], _lower_bound(lbp_ref), hg)
    project(HG_W, 2 * HG_W)
    operands()
    project(g0, g0 + D_MODEL)
    rotate_queries()

    kk_cols = slice(PK_KK, PK_KK + 4 * LANES)
    vv_cols = slice(PK_VV, PK_VV + 4 * LANES)
    kk_all = jnp.concatenate(
        [pk_prev[0, :, kk_cols], pk_main[0, :, kk_cols], pk_next[0, :, kk_cols]], axis=0)
    vv_all = jnp.concatenate(
        [pk_prev[0, :, vv_cols], pk_main[0, :, vv_cols], pk_next[0, :, vv_cols]], axis=0)
    r_i = lax.broadcasted_iota(jnp.int32, (ATT_BLOCK, KEY_SPAN), 0)
    c_i = lax.broadcasted_iota(jnp.int32, (ATT_BLOCK, KEY_SPAN), 1)
    off = c_i - r_i
    lane = lax.broadcasted_iota(jnp.int32, (1, LANES), 1)
    first = lane < ATT_HEAD_DIM
    seq_len = pl.num_programs(1) * SEQ_TILE
    seg_of_row = lax.broadcasted_iota(jnp.int32, (2 * KEY_SPAN, LANES), 0) // KEY_SPAN
    seg_of_lane = lax.broadcasted_iota(jnp.int32, (2 * KEY_SPAN, LANES), 1) // ATT_HEAD_DIM
    ones_bd = jnp.where(seg_of_row == seg_of_lane, 1.0, 0.0).astype(BF16)
    units = [(j, g) for j in range(Q_BLOCKS) for g in range(ATT_KV_HEADS)]

    def scores(j, g):
        rows = slice(j * ATT_BLOCK, (j + 1) * ATT_BLOCK)
        keys = slice(j * ATT_BLOCK, j * ATT_BLOCK + KEY_SPAN)
        ca = slice(2 * g * LANES, (2 * g + 1) * LANES)
        cb = slice((2 * g + 1) * LANES, (2 * g + 2) * LANES)
        lhs = jnp.concatenate([qrot_s[rows, ca], qrot_s[rows, cb]], axis=0)
        k_bd = jnp.concatenate([kk_all[keys, ca], kk_all[keys, cb]], axis=0)
        return _dot_nt(lhs, k_bd)

    pending = [scores(*units[0])]

    def attend(u):
        j, g = units[u]
        s = pending.pop()
        if u + 1 < len(units):
            pending.append(scores(*units[u + 1]))
        kstart = ti * SEQ_TILE + (j - 1) * ATT_BLOCK
        kpos = c_i + kstart
        valid = (off >= 0) & (off <= 2 * WINDOW) & (kpos >= 0) & (kpos < seq_len)
        bias = jnp.where(valid, 0.0, -jnp.inf)
        rows = slice(j * ATT_BLOCK, (j + 1) * ATT_BLOCK)
        keys = slice(j * ATT_BLOCK, j * ATT_BLOCK + KEY_SPAN)
        ca = slice(2 * g * LANES, (2 * g + 1) * LANES)
        cb = slice((2 * g + 1) * LANES, (2 * g + 2) * LANES)
        v_bd = jnp.concatenate([vv_all[keys, ca], vv_all[keys, cb]], axis=0)
        v_ext = jnp.concatenate([v_bd, ones_bd], axis=1)
        p_rows, sink_rows = [], []
        for half in range(2):
            p_seg, sink_seg = [], []
            for seg in range(2):
                head = 4 * g + 2 * half + seg
                ss = s[half * ATT_BLOCK:(half + 1) * ATT_BLOCK,
                       seg * KEY_SPAN:(seg + 1) * KEY_SPAN] + bias
                sk = sink_ref[head] * LOG2E
                m = jnp.maximum(jnp.max(ss, axis=-1, keepdims=True), sk)
                p_seg.append(jnp.exp2(ss - m).astype(BF16))
                sink_seg.append(jnp.exp2(sk - m))
            p_rows.append(jnp.concatenate(p_seg, axis=1))
            sink_rows.append(jnp.where(first, sink_seg[0], sink_seg[1]))
        pv = _dot(jnp.concatenate(p_rows, axis=0), v_ext)
        den = pv[:, LANES:] + jnp.concatenate(sink_rows, axis=0)
        out = pv[:, :LANES] / den
        att_s[rows, ca] = out[:ATT_BLOCK].astype(BF16)
        att_s[rows, cb] = out[ATT_BLOCK:].astype(BF16)

    for u in range(len(units)):
        attend(u)
    local()
    project(g0 + D_MODEL, g0 + 2 * D_MODEL)
    recurrence()
    outputs()
    y_att = _dot(att_s[...], watt_ref[...])
    rec_in = recurrent_branch()
    y_rec = _dot(rec_in, wrec_ref[...])

    merged = (_sigmoid(proj_s[:, g0:g0 + D_MODEL]) * y_rec
              + _sigmoid(proj_s[:, g0 + D_MODEL:g0 + 2 * D_MODEL]) * y_att)
    h_ref[0] = x + _dot(merged.astype(BF16), wout_ref[...])


def _mix_fwd(sink, x, cos, sin, ob, pk, g, w, lbp, hgn, wrec, watt, wout):
    b, s, _ = x.shape
    nt = s // SEQ_TILE
    r = SEQ_TILE // ATT_BLOCK
    nblk = s // ATT_BLOCK
    main = lambda bi, i: (bi, i, 0)
    prev = lambda bi, i: (bi, jnp.maximum(i * r - 1, 0), 0)
    nxt = lambda bi, i: (bi, jnp.minimum((i + 1) * r, nblk - 1), 0)
    halo = lambda imap: pl.BlockSpec((1, ATT_BLOCK, PK_HALO_W), imap)
    tile = lambda w_: pl.BlockSpec((1, SEQ_TILE, w_), main)
    rope = pl.BlockSpec((1, ROPE_ROWS, LANES), main)
    return pl.pallas_call(
        _mix_fwd_kernel,
        grid=(b, nt),
        in_specs=[
            pl.BlockSpec(memory_space=pltpu.SMEM),
            tile(D_MODEL), rope, rope, tile(HG_W),
            halo(prev), tile(PK_W), halo(nxt),
            _const_spec((1, D_MODEL)),
            _const_spec((D_MODEL, FWD_COLS)),
            _const_spec(lbp.shape),
            _const_spec((1, HG_W)),
            _const_spec((HG_W, D_MODEL)),
            _const_spec((ATT_QW, D_MODEL)),
            _const_spec((D_MODEL, D_MODEL)),
        ],
        out_specs=tile(D_MODEL),
        out_shape=jax.ShapeDtypeStruct((b, s, D_MODEL), F32),
        scratch_shapes=[
            pltpu.VMEM((SEQ_TILE, FWD_COLS), F32),
            pltpu.VMEM((SEQ_TILE, HG_W), F32),
            pltpu.VMEM((SEQ_TILE, ATT_QW), BF16),
            pltpu.VMEM((SEQ_TILE, ATT_QW), BF16),
        ] + _hgrn_scratch(),
        compiler_params=pltpu.CompilerParams(
            dimension_semantics=("arbitrary", "arbitrary"), vmem_limit_bytes=VMEM_LIMIT_BYTES),
        name="mix_fwd",
    )(sink, x, cos, sin, ob, pk, pk, pk, g, w, lbp, hgn, wrec, watt, wout)


def _mem_xattn_kernel(h_ref, mem_ref, g_ref, gkv_ref, wq_ref, wkv_ref, wo_ref, o_ref,
                      att_s, k_s, v_s):
    @pl.when(pl.program_id(1) == 0)
    def _():
        kv = _dot(_rms(mem_ref[0], gkv_ref[...]).astype(BF16), wkv_ref[...])
        k_s[...] = kv[:, :D_MODEL].astype(BF16)
        v_s[...] = kv[:, D_MODEL:].astype(BF16)

    h = h_ref[0]
    n = _rms(h, g_ref[...]).astype(BF16)
    q = (_dot(n, wq_ref[...]) * (MEM_HEAD_DIM ** -0.5 * LOG2E)).astype(BF16)
    head = lambda hd: slice(hd * MEM_HEAD_DIM, (hd + 1) * MEM_HEAD_DIM)
    scores = lambda hd: _dot_nt(q[:, head(hd)], k_s[:, head(hd)])
    ahead = scores(0)
    for hd in range(MEM_HEADS):
        sl = head(hd)
        s = ahead
        if hd + 1 < MEM_HEADS:
            ahead = scores(hd + 1)
        m = jnp.max(s, axis=-1, keepdims=True)
        p = jnp.exp2(s - m)
        inv = 1.0 / jnp.sum(p, axis=-1, keepdims=True)
        att_s[:, sl] = (_dot(p.astype(BF16), v_s[:, sl]) * inv).astype(BF16)
    o_ref[0] = h + _dot(att_s[...], wo_ref[...])


def _mem_xattn(h, mem, g, g_kv, wq, w_kv, wo):
    b, s, _ = h.shape
    tile = pl.BlockSpec((1, SEQ_TILE, D_MODEL), lambda bi, i: (bi, i, 0))
    mem_blk = pl.BlockSpec((1, MEM_LEN, D_MODEL), lambda bi, i: (bi, 0, 0))
    return pl.pallas_call(
        _mem_xattn_kernel,
        grid=(b, s // SEQ_TILE),
        in_specs=[tile, mem_blk, _const_spec((1, D_MODEL)), _const_spec((1, D_MODEL)),
                  _const_spec((D_MODEL, D_MODEL)), _const_spec((D_MODEL, 2 * D_MODEL)),
                  _const_spec((D_MODEL, D_MODEL))],
        out_specs=tile,
        out_shape=jax.ShapeDtypeStruct((b, s, D_MODEL), F32),
        scratch_shapes=[pltpu.VMEM((SEQ_TILE, D_MODEL), BF16),
                        pltpu.VMEM((MEM_LEN, D_MODEL), BF16),
                        pltpu.VMEM((MEM_LEN, D_MODEL), BF16)],
        compiler_params=pltpu.CompilerParams(
            dimension_semantics=("arbitrary", "arbitrary"), vmem_limit_bytes=VMEM_LIMIT_BYTES),
        name="mem_xattn",
    )(h, mem, g, g_kv, wq, w_kv, wo)


FFN_ROWS = SEQ_TILE + 2 * SUBLANES_F32


def _conv_ffn_kernel(hp_ref, h_ref, hn_ref, g_ref, wu_ref, wg_ref, cw_ref, cb_ref, wd_ref, fn_ref,
                     o_ref):
    ti = pl.program_id(1)
    h = h_ref[0]
    hcat = jnp.concatenate([hp_ref[0], h, hn_ref[0]], axis=0)
    nf = _rms(hcat, g_ref[...])
    n_all = nf.astype(BF16)
    n_main = nf[SUBLANES_F32:SUBLANES_F32 + SEQ_TILE].astype(BF16)
    ridx = lax.broadcasted_iota(jnp.int32, (FFN_ROWS, 1), 0)
    pad_lo = (ridx < SUBLANES_F32) & (ti == 0)
    pad_hi = (ridx >= SUBLANES_F32 + SEQ_TILE) & (ti == pl.num_programs(1) - 1)
    rowmask = jnp.where(pad_lo | pad_hi, 0.0, 1.0)
    main = slice(SUBLANES_F32, SUBLANES_F32 + SEQ_TILE)
    starts = [sum(FF_CHUNKS[:i]) for i in range(len(FF_CHUNKS))]
    col_slices = [slice(c0, c0 + w) for c0, w in zip(starts, FF_CHUNKS)]

    def up(cols):
        return _dot(n_main, wu_ref[:, cols]), _dot(n_all, wg_ref[:, cols]) * rowmask

    acc = jnp.zeros((SEQ_TILE, D_MODEL), F32)
    ahead = up(col_slices[0])
    for i, cols in enumerate(col_slices):
        u, gx = ahead
        if i + 1 < len(col_slices):
            ahead = up(col_slices[i + 1])
        g_prev = pltpu.roll(gx, 1, axis=0)[main]
        g_next = pltpu.roll(gx, FFN_ROWS - 1, axis=0)[main]
        gc = (g_prev * cw_ref[0:1, cols] + gx[main] * cw_ref[1:2, cols]
              + g_next * cw_ref[2:3, cols] + cb_ref[:, cols])
        a = (gc * _sigmoid(gc) * u).astype(BF16)
        acc = acc + _dot(a, wd_ref[cols, :])
    o_ref[0] = _rms(h + acc, fn_ref[...])


def _conv_ffn(h, g, wu, wg, cw, cb, wd, fn):
    b, s, _ = h.shape
    r = SEQ_TILE // SUBLANES_F32
    nblk = s // SUBLANES_F32
    tile = pl.BlockSpec((1, SEQ_TILE, D_MODEL), lambda bi, i: (bi, i, 0))
    prev = pl.BlockSpec((1, SUBLANES_F32, D_MODEL), lambda bi, i: (bi, jnp.maximum(i * r - 1, 0), 0))
    nxt = pl.BlockSpec((1, SUBLANES_F32, D_MODEL),
                       lambda bi, i: (bi, jnp.minimum((i + 1) * r, nblk - 1), 0))
    return pl.pallas_call(
        _conv_ffn_kernel,
        grid=(b, s // SEQ_TILE),
        in_specs=[prev, tile, nxt, _const_spec((1, D_MODEL)),
                  _const_spec((D_MODEL, D_FF)), _const_spec((D_MODEL, D_FF)),
                  _const_spec((3, D_FF)), _const_spec((1, D_FF)),
                  _const_spec((D_FF, D_MODEL)), _const_spec((1, D_MODEL))],
        out_specs=tile,
        out_shape=jax.ShapeDtypeStruct((b, s, D_MODEL), F32),
        compiler_params=pltpu.CompilerParams(
            dimension_semantics=("arbitrary", "arbitrary"), vmem_limit_bytes=VMEM_LIMIT_BYTES),
        name="conv_ffn",
    )(h, h, h, g, wu, wg, cw, cb, wd, fn)


def kernel(x, mem, positions, norm_mix, w_in, lower_bounds, hg_norm, attn_sink, w_br_rec, w_br_att,
           w_mix_out, norm_mem, norm_mem_kv, w_mem_q, w_mem_kv, w_mem_o, norm_ffn, w_up, conv_w,
           conv_b, w_down, final_norm):
    assert x.shape[1] % SEQ_TILE == 0 and x.shape[2] == D_MODEL and w_in.shape[0] == 1
    bf = lambda a: a.astype(BF16)
    row = lambda a: a.reshape(1, -1).astype(F32)

    o = 0
    cols = {}
    for name, width in (("q_r", HG_W), ("fz_f", HG_W), ("fz_b", HG_W), ("i_r", HG_W), ("g_r", HG_W),
                        ("q_a", ATT_QW), ("k_a", ATT_KVW), ("v_a", ATT_KVW),
                        ("gate_r", D_MODEL), ("gate_a", D_MODEL)):
        cols[name] = w_in[0][:, o:o + width]
        o += width
    w_bwd = bf(jnp.concatenate([cols[k] for k in ("q_r", "fz_b", "i_r", "k_a", "v_a")], axis=1))
    w_fwd = bf(jnp.concatenate(
        [cols[k] for k in ("fz_f", "g_r", "q_a", "gate_r", "gate_a")], axis=1))

    cos, sin = _rope_tables(positions)
    g_mix = row(norm_mix[0])

    o_bwd, pk = _mix_bwd(x, cos, sin, g_mix, w_bwd, lower_bounds[1].astype(F32))
    h1 = _mix_fwd(attn_sink[0].astype(F32), x, cos, sin, o_bwd, pk, g_mix, w_fwd,
                  lower_bounds[0].astype(F32), row(hg_norm[0]), bf(w_br_rec[0]), bf(w_br_att[0]),
                  bf(w_mix_out[0]))
    h2 = _mem_xattn(h1, mem, row(norm_mem[0]), row(norm_mem_kv[0]), bf(w_mem_q[0]),
                    bf(w_mem_kv[0]), bf(w_mem_o[0]))
    return _conv_ffn(h2, row(norm_ffn[0]), bf(w_up[0][:, :D_FF]), bf(w_up[0][:, D_FF:]),
                     conv_w[0].astype(F32), row(conv_b[0]), bf(w_down[0]), row(final_norm))
```
